```python
import math
import jax
import jax.numpy as jnp
from jax import lax
import numpy as np

D_MODEL = 2048
BATCH = 1
SEQ = 8192
DEPTH = 4

GRID_W = 64
CTX_LEN = 256
D_MIX = D_MODEL
A_HEADS = 8
A_HEAD_DIM = 64
A_VAL_DIM = 2 * A_HEAD_DIM
QK_A = A_HEADS * 2 * A_HEAD_DIM
V_A = A_HEADS * A_VAL_DIM
B_HEADS = 8
B_DK = 128
B_DV = 128
QK_B = B_HEADS * B_DK
V_B = B_HEADS * B_DV
CONV_K = 5
CONV_CH = 2 * QK_B + V_B
CHUNK = 64
Q_BLOCK = 128
ROPE_THETA = 10000.0
NORM_EPS = 1e-6
SUBLN_EPS = 1e-5
N_IN = 2 * QK_A + 2 * V_A + CONV_CH + V_B + 4 * B_HEADS

kernel_name = 'hybrid_diffattn_gdn_prefix_trunk'


def rmsnorm(x, w, eps=NORM_EPS):
    xf = x.astype(jnp.float32)
    y = xf * lax.rsqrt(jnp.mean(xf * xf, axis=-1, keepdims=True) + eps)
    return (y * w.astype(jnp.float32)).astype(x.dtype)


def l2norm(x, eps=1e-6):
    xf = x.astype(jnp.float32)
    return xf * lax.rsqrt(jnp.sum(xf * xf, axis=-1, keepdims=True) + eps)


def split_in(p):
    sizes = (QK_A, QK_A, V_A, V_A, CONV_CH, V_B, 4 * B_HEADS)
    points = np.cumsum(sizes)[:-1].tolist()
    return jnp.split(p, points, axis=-1)


def axial_rope(rows, cols):
    n_freq = A_HEAD_DIM // 4
    inv = ROPE_THETA ** (-jnp.arange(n_freq, dtype=jnp.float32) / n_freq)
    ang = jnp.concatenate([rows[:, None] * inv, cols[:, None] * inv], axis=-1)
    return jnp.cos(ang), jnp.sin(ang)


def apply_rope(t, cos, sin):
    t1, t2 = jnp.split(t, 2, axis=-1)
    c = cos[None, :, None, :]
    s = sin[None, :, None, :]
    return jnp.concatenate([t1 * c - t2 * s, t1 * s + t2 * c], axis=-1).astype(t.dtype)


def qk_heads(t, cos=None, sin=None):
    bsz, n, _ = t.shape
    t = t.reshape(bsz, n, A_HEADS * 2, A_HEAD_DIM)
    if cos is not None:
        t = apply_rope(t, cos, sin)
    return t.reshape(bsz, n, A_HEADS, 2, A_HEAD_DIM).transpose(0, 2, 3, 1, 4)


def v_heads(t):
    bsz, n, _ = t.shape
    return t.reshape(bsz, n, A_HEADS, A_VAL_DIM).transpose(0, 2, 1, 3)


def diff_attend(q, k, v, lam):
    s = jnp.einsum('bhmqd,bhmkd->bhmqk', q, k).astype(jnp.float32) * (A_HEAD_DIM ** -0.5)
    p = jax.nn.softmax(s, axis=-1)
    a = p[:, :, 0] - lam * p[:, :, 1]
    return jnp.einsum('bhqk,bhkd->bhqd', a.astype(v.dtype), v)


def diff_attend_blocked(q, k, v, lam):
    bsz, h, _, n_q, d = q.shape
    nb = n_q // Q_BLOCK
    qb = q.reshape(bsz, h, 2, nb, Q_BLOCK, d).transpose(3, 0, 1, 2, 4, 5)
    ob = lax.map(lambda qq: diff_attend(qq, k, v, lam), qb)
    return ob.transpose(1, 2, 0, 3, 4).reshape(bsz, h, n_q, A_VAL_DIM)


def diff_post(o, subln_w, lam_init, z):
    bsz, _, n, _ = o.shape
    o = rmsnorm(o, subln_w, SUBLN_EPS) * (1.0 - lam_init)
    o = o.transpose(0, 2, 1, 3).reshape(bsz, n, V_A)
    return (o * jax.nn.silu(z)).astype(z.dtype)


def centred_dwconv(x, w):
    k = w.shape[0]
    return lax.conv_general_dilated(
        x, w[:, None, :].astype(x.dtype), window_strides=(1,),
        padding=[((k - 1) // 2, k // 2)],
        dimension_numbers=('NWC', 'WIO', 'NWC'),
        feature_group_count=x.shape[-1])


def delta_prep(qkv, ab, conv_w, a_log, dt_bias):
    bsz, n, _ = qkv.shape
    qkv = jax.nn.silu(centred_dwconv(qkv, conv_w))
    q, k, v = jnp.split(qkv, [QK_B, 2 * QK_B], axis=-1)
    q = l2norm(q.reshape(bsz, n, B_HEADS, B_DK)).transpose(0, 2, 1, 3)
    k = l2norm(k.reshape(bsz, n, B_HEADS, B_DK)).transpose(0, 2, 1, 3)
    v = v.reshape(bsz, n, B_HEADS, B_DV).transpose(0, 2, 1, 3).astype(jnp.float32)
    ab = ab.astype(jnp.float32).reshape(bsz, n, 2, 2, B_HEADS).transpose(2, 3, 0, 4, 1)
    a_log = a_log.astype(jnp.float32)
    dt_bias = dt_bias.astype(jnp.float32)
    g = -jnp.exp(a_log)[:, None, :, None] * jax.nn.softplus(ab[0] + dt_bias[:, None, :, None])
    beta = jax.nn.sigmoid(ab[1])
    return q, k, v, g, beta


def gated_delta_chunked(q, k, v, g, beta, s0):
    bsz, h, n, dk = q.shape
    dv = v.shape[-1]
    nc = n // CHUNK
    q = q * (dk ** -0.5)
    rs = lambda t: t.reshape(bsz, h, nc, CHUNK, *t.shape[3:])
    q, k, v, g, beta = rs(q), rs(k), rs(v), rs(g), rs(beta)
    g = jnp.cumsum(g, axis=-1)
    tri = jnp.tril(jnp.ones((CHUNK, CHUNK), dtype=bool))
    tri_strict = jnp.tril(jnp.ones((CHUNK, CHUNK), dtype=bool), -1)
    diff = g[..., :, None] - g[..., None, :]
    decay = jnp.where(tri, jnp.exp(jnp.where(tri, diff, 0.0)), 0.0)
    k_beta = k * beta[..., None]
    m = jnp.einsum('bhncd,bhnsd->bhncs', k_beta, k) * decay
    m = jnp.where(tri_strict, m, 0.0) + jnp.eye(CHUNK, dtype=jnp.float32)
    rhs = jnp.concatenate([v * beta[..., None], k_beta * jnp.exp(g)[..., None]], axis=-1)
    sol = lax.linalg.triangular_solve(m, rhs, left_side=True, lower=True, unit_diagonal=True)
    u, w = sol[..., :dv], sol[..., dv:]
    attn = jnp.einsum('bhncd,bhnsd->bhncs', q, k) * decay
    g_last = g[..., -1]
    k_end = k * jnp.exp(g_last[..., None] - g)[..., None]
    q_dec = q * jnp.exp(g)[..., None]

    def step(state, xs):
        u_c, w_c, qd_c, at_c, ke_c, gl_c = xs
        v_new = u_c - jnp.einsum('bhcd,bhde->bhce', w_c, state)
        o_c = jnp.einsum('bhcd,bhde->bhce', qd_c, state) + jnp.einsum('bhcs,bhse->bhce', at_c, v_new)
        state = state * jnp.exp(gl_c)[..., None, None] + jnp.einsum('bhcd,bhce->bhde', ke_c, v_new)
        return state, o_c

    xs = tuple(jnp.moveaxis(t, 2, 0) for t in (u, w, q_dec, attn, k_end, g_last))
    s_final, o = lax.scan(step, s0, xs)
    o = jnp.moveaxis(o, 0, 2).reshape(bsz, h, n, dv)
    return o, s_final


def delta_bidirectional(lat, ctx_side):
    q, k, v, g, beta = lat
    qc, kc, vc, gc, bc = ctx_side
    s0 = jnp.zeros((q.shape[0], B_HEADS, B_DK, B_DV), jnp.float32)
    flip = lambda t: jnp.flip(t, axis=2)
    oc_f, s_f = gated_delta_chunked(qc, kc, vc, gc[0], bc[0], s0)
    o_f, _ = gated_delta_chunked(q, k, v, g[0], beta[0], s_f)
    oc_b, s_b = gated_delta_chunked(flip(qc), flip(kc), flip(vc), flip(gc[1]), flip(bc[1]), s0)
    o_b, _ = gated_delta_chunked(flip(q), flip(k), flip(v), flip(g[1]), flip(beta[1]), s_b)
    return o_f + flip(o_b), oc_f + flip(oc_b)


def delta_post(o, dn_norm_w, z):
    bsz, _, n, _ = o.shape
    o = rmsnorm(o.transpose(0, 2, 1, 3), dn_norm_w)
    zf = jax.nn.silu(z.reshape(bsz, n, B_HEADS, B_DV).astype(jnp.float32))
    return (o * zf).reshape(bsz, n, V_B).astype(z.dtype)


def hybrid_layer(x, ctx, c, c_ctx, cos, sin, lam_init, w_ada, b_ada, norm_w, w_in, conv_w,
                 a_log, dt_bias, dn_norm_w, lam_p, subln_w, w_out, update_ctx):
    mod_lat = jax.nn.silu(c) @ w_ada + b_ada
    mod_ctx = jax.nn.silu(c_ctx) @ w_ada + b_ada
    sh, sc, gt = jnp.split(mod_lat[:, None, :], 3, axis=-1)
    sh_c, sc_c, gt_c = jnp.split(mod_ctx, 3, axis=-1)
    p = (rmsnorm(x, norm_w) * (1.0 + sc) + sh) @ w_in
    pc = (rmsnorm(ctx, norm_w) * (1.0 + sc_c) + sh_c) @ w_in
    qa, ka, va, za, qkvb, zb, ab = split_in(p)
    qa_c, ka_c, va_c, za_c, qkvb_c, zb_c, ab_c = split_in(pc)

    lp = lam_p.astype(jnp.float32)
    lam = jnp.exp(jnp.sum(lp[0] * lp[1])) - jnp.exp(jnp.sum(lp[2] * lp[3])) + lam_init
    q_l = qk_heads(qa, cos, sin)
    k_l = qk_heads(ka, cos, sin)
    k_c = qk_heads(ka_c)
    v_c = v_heads(va_c)
    k_all = jnp.concatenate([k_l, k_c], axis=3)
    v_all = jnp.concatenate([v_heads(va), v_c], axis=2)
    y_a = diff_post(diff_attend_blocked(q_l, k_all, v_all, lam), subln_w, lam_init, za)

    lat_in = delta_prep(qkvb, ab, conv_w, a_log, dt_bias)
    ctx_in = delta_prep(qkvb_c, ab_c, conv_w, a_log, dt_bias)
    o_b, o_bc = delta_bidirectional(lat_in, ctx_in)
    y_b = delta_post(o_b, dn_norm_w, zb)

    x = x + gt * (jnp.concatenate([y_a, y_b], axis=-1) @ w_out)
    if update_ctx:
        y_ac = diff_post(diff_attend(qk_heads(qa_c), k_c, v_c, lam), subln_w, lam_init, za_c)
        y_bc = delta_post(o_bc, dn_norm_w, zb_c)
        ctx = ctx + gt_c * (jnp.concatenate([y_ac, y_bc], axis=-1) @ w_out)
    return x, ctx


def setup_inputs(seed: int = 0) -> dict:
    key = jax.random.key(seed)
    ks = jax.random.split(key, 16)
    f32 = jnp.float32
    nrm = lambda k, shape, s: jax.random.normal(k, shape, f32) * s
    x = nrm(ks[0], (BATCH, SEQ, D_MODEL), 1.0)
    c = nrm(ks[1], (BATCH, D_MODEL), 1.0)
    ctx = nrm(ks[2], (BATCH, CTX_LEN, D_MODEL), 1.0)
    c_ctx = nrm(ks[3], (D_MODEL,), 1.0)
    w_ada = nrm(ks[4], (DEPTH, D_MODEL, 3 * D_MODEL), 0.5 * D_MODEL ** -0.5)
    b_ada = nrm(ks[5], (DEPTH, 3 * D_MODEL), 0.02)
    norm_w = 1.0 + nrm(ks[6], (DEPTH, D_MODEL), 0.02)
    w_in = nrm(ks[7], (DEPTH, D_MODEL, N_IN), D_MODEL ** -0.5)
    conv_w = nrm(ks[8], (DEPTH, CONV_K, CONV_CH), CONV_K ** -0.5)
    dn_a_log = jnp.log(jax.random.uniform(ks[9], (DEPTH, 2, B_HEADS), f32, 1.0, 16.0))
    dt = jnp.exp(jax.random.uniform(ks[10], (DEPTH, 2, B_HEADS), f32, math.log(1e-3), math.log(1e-1)))
    dn_dt_bias = dt + jnp.log(-jnp.expm1(-dt))
    dn_norm_w = 1.0 + nrm(ks[11], (DEPTH, B_DV), 0.02)
    diff_lambda = nrm(ks[12], (DEPTH, 4, A_HEAD_DIM), 0.1)
    subln_w = 1.0 + nrm(ks[13], (DEPTH, A_VAL_DIM), 0.02)
    w_out = nrm(ks[14], (DEPTH, D_MIX, D_MODEL), D_MIX ** -0.5)
    final_norm_w = 1.0 + nrm(ks[15], (D_MODEL,), 0.02)
    return {'x': x, 'c': c, 'ctx': ctx, 'c_ctx': c_ctx, 'w_ada': w_ada, 'b_ada': b_ada,
            'norm_w': norm_w, 'w_in': w_in, 'conv_w': conv_w, 'dn_a_log': dn_a_log,
            'dn_dt_bias': dn_dt_bias, 'dn_norm_w': dn_norm_w, 'diff_lambda': diff_lambda,
            'subln_w': subln_w, 'w_out': w_out, 'final_norm_w': final_norm_w}


def reference(x, c, ctx, c_ctx, w_ada, b_ada, norm_w, w_in, conv_w, dn_a_log, dn_dt_bias,
              dn_norm_w, diff_lambda, subln_w, w_out, final_norm_w):
    n_lat = x.shape[1]
    n_rows = n_lat // GRID_W
    rows = jnp.repeat(jnp.arange(n_rows, dtype=jnp.float32), GRID_W)
    cols = jnp.tile(jnp.arange(GRID_W, dtype=jnp.float32), n_rows)
    cos, sin = axial_rope(rows, cols)
    for i in range(DEPTH):
        lam_init = 0.8 - 0.6 * math.exp(-0.3 * i)
        x, ctx = hybrid_layer(x, ctx, c, c_ctx, cos, sin, lam_init, w_ada[i], b_ada[i],
                              norm_w[i], w_in[i], conv_w[i], dn_a_log[i], dn_dt_bias[i],
                              dn_norm_w[i], diff_lambda[i], subln_w[i], w_out[i],
                              update_ctx=(i < DEPTH - 1))
    return rmsnorm(x, final_norm_w)
```

```python
import functools
import math

import jax
import jax.numpy as jnp
from jax import lax
from jax.experimental import pallas as pl
from jax.experimental.pallas import tpu as pltpu

F32 = jnp.float32
BF16 = jnp.bfloat16

LANES = 128
VMEM_LIMIT_BYTES = 56 * 1024 * 1024

D_MODEL = 2048
GRID_W = 64
A_HEADS = 8
A_HEAD_DIM = 64
A_VAL_DIM = 2 * A_HEAD_DIM
QK_A = A_HEADS * 2 * A_HEAD_DIM
V_A = A_HEADS * A_VAL_DIM
B_HEADS = 8
B_DK = 128
B_DV = 128
QK_B = B_HEADS * B_DK
V_B = B_HEADS * B_DV
CONV_K = 5
CONV_CH = 2 * QK_B + V_B
ROPE_THETA = 10000.0
NORM_EPS = 1e-6
SUBLN_EPS = 1e-5
L2_EPS = 1e-6

DELTA_CHUNK = 128
V_ROWS = A_VAL_DIM + 16
NEG_BIG = -1e30


def _params(*sem):
    return pltpu.CompilerParams(dimension_semantics=sem, vmem_limit_bytes=VMEM_LIMIT_BYTES)


def _silu(x):
    return x * jax.nn.sigmoid(x)


def _tile(n, preferred, fallback):
    return preferred if n % preferred == 0 else fallback


def _ada_kernel(c_ref, w_ref, b_ref, o_ref):
    s = _silu(c_ref[...])
    o_ref[0] = jnp.dot(s.astype(BF16), w_ref[0].astype(BF16), preferred_element_type=F32) + b_ref[0]


def _ada_all(cc, w_ada, b_ada):
    depth, d, n = w_ada.shape
    tn = 512
    return pl.pallas_call(
        _ada_kernel,
        grid=(depth, n // tn),
        in_specs=[pl.BlockSpec((8, d), lambda l, j: (0, 0)),
                  pl.BlockSpec((1, d, tn), lambda l, j: (l, 0, j)),
                  pl.BlockSpec((1, 1, tn), lambda l, j: (l, 0, j))],
        out_specs=pl.BlockSpec((1, 8, tn), lambda l, j: (l, 0, j)),
        out_shape=jax.ShapeDtypeStruct((depth, 8, n), F32),
        compiler_params=_params("parallel", "parallel"),
    )(cc, w_ada, b_ada.reshape(depth, 1, n))


def _modulated_norm(x, nw, shift, scale):
    ms = jnp.mean(x * x, axis=-1, keepdims=True)
    return (x * lax.rsqrt(ms + NORM_EPS) * nw) * (1.0 + scale) + shift


def _norm_kernel(x_ref, nw_ref, mod_ref, o_ref, *, n_lat_tiles):
    is_ctx = pl.program_id(0) >= n_lat_tiles
    d = x_ref.shape[1]
    shift = jnp.where(is_ctx, mod_ref[1:2, 0:d], mod_ref[0:1, 0:d])
    scale = jnp.where(is_ctx, mod_ref[1:2, d:2 * d], mod_ref[0:1, d:2 * d])
    o_ref[...] = _modulated_norm(x_ref[...], nw_ref[...], shift, scale).astype(o_ref.dtype)


def _norm_mod(xcat, nw, mod, n_lat):
    t, d = xcat.shape
    tm = 256
    return pl.pallas_call(
        functools.partial(_norm_kernel, n_lat_tiles=n_lat // tm),
        grid=(t // tm,),
        in_specs=[pl.BlockSpec((tm, d), lambda i: (i, 0)),
                  pl.BlockSpec((1, d), lambda i: (0, 0)),
                  pl.BlockSpec((8, 2 * d), lambda i: (0, 0))],
        out_specs=pl.BlockSpec((tm, d), lambda i: (i, 0)),
        out_shape=jax.ShapeDtypeStruct((t, d), BF16),
        compiler_params=_params("parallel"),
    )(xcat, nw, mod)


def _mm_kernel(x_ref, w_ref, o_ref):
    o_ref[...] = jnp.dot(x_ref[...], w_ref[...], preferred_element_type=F32).astype(o_ref.dtype)


def _mm_rope_kernel(x_ref, w_ref, cos_ref, sin_ref, o_ref):
    t = jnp.dot(x_ref[...], w_ref[...], preferred_element_type=F32)
    c = cos_ref[...]
    s = sin_ref[...]
    lane = lax.broadcasted_iota(jnp.int32, c.shape, 1)
    first = (lane % A_HEAD_DIM) < (A_HEAD_DIM // 2)
    for a in range(0, t.shape[1], LANES):
        th = t[:, a:a + LANES]
        sw = jnp.where(first, pltpu.roll(th, LANES - A_HEAD_DIM // 2, 1), pltpu.roll(th, A_HEAD_DIM // 2, 1))
        o_ref[:, a:a + LANES] = (th * c + sw * s).astype(o_ref.dtype)


def _proj(xn, w, out_dtype, tn, rope=None):
    t, d = xn.shape
    n = w.shape[1]
    tm = _tile(t, 1408, 256)
    in_specs = [pl.BlockSpec((tm, d), lambda i, j: (i, 0)),
                pl.BlockSpec((d, tn), lambda i, j: (0, j))]
    args = [xn, w]
    kern = _mm_kernel
    if rope is not None:
        in_specs += [pl.BlockSpec((tm, LANES), lambda i, j: (i, 0))] * 2
        args += list(rope)
        kern = _mm_rope_kernel
    return pl.pallas_call(
        kern,
        grid=(t // tm, n // tn),
        in_specs=in_specs,
        out_specs=pl.BlockSpec((tm, tn), lambda i, j: (i, j)),
        out_shape=jax.ShapeDtypeStruct((t, n), out_dtype),
        compiler_params=_params("parallel", "parallel"),
    )(*args)


def _attn_kernel(q_ref, k_ref, vt_ref, z_ref, lam_ref, sw_ref, o_ref, *, tk, lam_init):
    tq = q_ref.shape[0]
    nkeys = k_ref.shape[0]
    q = q_ref[...]
    lane = lax.broadcasted_iota(jnp.int32, q.shape, 1)
    zero = jnp.zeros_like(q)
    qz = jnp.concatenate([jnp.where(lane < A_HEAD_DIM, q, zero),
                          jnp.where(lane >= A_HEAD_DIM, q, zero)], axis=0)
    m = jnp.full((1, 2 * tq), NEG_BIG, F32)
    acc = jnp.zeros((V_ROWS, 2 * tq), F32)
    for c0 in range(0, nkeys, tk):
        kc = k_ref[c0:c0 + tk, :]
        st = lax.dot_general(kc, qz, (((1,), (1,)), ((), ())), preferred_element_type=F32)
        m_new = jnp.maximum(m, jnp.max(st, axis=0, keepdims=True))
        alpha = jnp.exp(m - m_new)
        p = jnp.exp(st - m_new).astype(BF16)
        acc = alpha * acc + jnp.dot(vt_ref[0, :, c0:c0 + tk], p, preferred_element_type=F32)
        m = m_new
    o1 = acc[0:A_VAL_DIM, 0:tq] / acc[A_VAL_DIM:A_VAL_DIM + 1, 0:tq]
    o2 = acc[0:A_VAL_DIM, tq:] / acc[A_VAL_DIM:A_VAL_DIM + 1, tq:]
    lp = lam_ref[...]
    lam = (jnp.exp(jnp.sum(lp[0:1] * lp[1:2], axis=1, keepdims=True))
           - jnp.exp(jnp.sum(lp[2:3] * lp[3:4], axis=1, keepdims=True)) + lam_init)
    o = (o1 - lam * o2).T
    ms = jnp.mean(o * o, axis=-1, keepdims=True)
    y = (o * lax.rsqrt(ms + SUBLN_EPS) * sw_ref[...]) * (1.0 - lam_init)
    o_ref[...] = (y * _silu(z_ref[...].astype(F32))).astype(o_ref.dtype)


def _attention(qk, vz, vt, lam_p, subln_w, lam_init, *, q_row0, n_q, k_row0, n_keys, tq, tk, ya=None):
    t = qk.shape[0]
    qb0 = q_row0 // tq
    kb0 = k_row0 // n_keys
    in_specs = [pl.BlockSpec((tq, LANES), lambda h, i: (qb0 + i, h)),
                pl.BlockSpec((n_keys, LANES), lambda h, i: (kb0, A_HEADS + h)),
                pl.BlockSpec((1, V_ROWS, n_keys), lambda h, i: (h, 0, kb0)),
                pl.BlockSpec((tq, LANES), lambda h, i: (qb0 + i, A_HEADS + h)),
                pl.BlockSpec(lam_p.shape, lambda h, i: (0, 0)),
                pl.BlockSpec((1, A_VAL_DIM), lambda h, i: (0, 0))]
    args = [qk, qk, vt, vz, lam_p, subln_w]
    aliases = {}
    if ya is not None:
        in_specs.append(pl.BlockSpec(memory_space=pl.ANY))
        args.append(ya)
        aliases = {len(args) - 1: 0}
    kern = functools.partial(_attn_kernel, tk=tk, lam_init=lam_init)
    if ya is not None:
        kern = lambda *refs: functools.partial(_attn_kernel, tk=tk, lam_init=lam_init)(*refs[:6], refs[7])
    return pl.pallas_call(
        kern,
        grid=(A_HEADS, n_q // tq),
        in_specs=in_specs,
        out_specs=pl.BlockSpec((tq, LANES), lambda h, i: (qb0 + i, h)),
        out_shape=jax.ShapeDtypeStruct((t, V_A), BF16),
        input_output_aliases=aliases,
        compiler_params=_params("parallel", "parallel"),
    )(*args)


def _conv_kernel(x_ref, w_ref, o_ref, pad_ref, *, n_lat, n_ctx, rows):
    j = pl.program_id(0)
    half = (CONV_K - 1) // 2
    lat0 = 8
    ctx0 = lat0 + n_lat + 16
    zeros8 = jnp.zeros((8, LANES), F32)
    pad_ref[0:8, :] = zeros8
    pad_ref[lat0 + n_lat:lat0 + n_lat + 8, :] = zeros8
    pad_ref[lat0 + n_lat + 8:ctx0, :] = zeros8
    pad_ref[ctx0 + n_ctx:ctx0 + n_ctx + 8, :] = zeros8
    pad_ref[lat0:lat0 + n_lat, :] = x_ref[0:n_lat, :]
    pad_ref[ctx0:ctx0 + n_ctx, :] = x_ref[n_lat:n_lat + n_ctx, :]
    w = w_ref[...]
    is_qk = j < 2 * B_HEADS
    post = jnp.where(j < B_HEADS, B_DK ** -0.5, 1.0)
    for src0, dst0, n in ((lat0, 0, n_lat), (ctx0, n_lat, n_ctx)):
        for r in range(0, n, rows):
            acc = jnp.zeros((rows, LANES), F32)
            for tap in range(CONV_K):
                a = src0 + r + tap - half
                acc = acc + pad_ref[a:a + rows, :] * w[tap:tap + 1, :]
            y = _silu(acc)
            nrm = lax.rsqrt(jnp.sum(y * y, axis=-1, keepdims=True) + L2_EPS) * post
            y = y * jnp.where(is_qk, nrm, 1.0)
            o_ref[dst0 + r:dst0 + r + rows, :] = y.astype(o_ref.dtype)


def _conv_prep(pb, conv_w, n_lat, n_ctx):
    t = pb.shape[0]
    nblk = CONV_CH // LANES
    return pl.pallas_call(
        functools.partial(_conv_kernel, n_lat=n_lat, n_ctx=n_ctx, rows=256),
        grid=(nblk,),
        in_specs=[pl.BlockSpec((t, LANES), lambda j: (0, j)),
                  pl.BlockSpec((CONV_K, LANES), lambda j: (0, j))],
        out_specs=pl.BlockSpec((t, LANES), lambda j: (0, j)),
        out_shape=jax.ShapeDtypeStruct((t, CONV_CH), BF16),
        scratch_shapes=[pltpu.VMEM((t + 40, LANES), F32)],
        compiler_params=_params("parallel"),
    )(pb, conv_w)


def _gates_kernel(ab_ref, alog_ref, dtb_ref, o_ref):
    c = DELTA_CHUNK
    nh = B_HEADS
    r = lax.broadcasted_iota(jnp.int32, (c, c), 0)
    l = lax.broadcasted_iota(jnp.int32, (c, c), 1)
    upper = (r <= l).astype(F32)
    lower = (r >= l).astype(F32)
    neg_a = -jnp.exp(alog_ref[...])
    dtb = dtb_ref[...]
    for t0 in range(0, ab_ref.shape[1], c):
        x = ab_ref[0:2 * nh, t0:t0 + c] + dtb
        sp = jnp.maximum(x, 0.0) + jnp.log1p(jnp.exp(-jnp.abs(x)))
        g = neg_a * sp
        o_ref[0:nh, t0:t0 + c] = jnp.dot(g[0:nh], upper, preferred_element_type=F32,
                                          precision=lax.Precision.HIGHEST)
        o_ref[nh:2 * nh, t0:t0 + c] = jnp.dot(g[nh:2 * nh], lower, preferred_element_type=F32,
                                               precision=lax.Precision.HIGHEST)
        o_ref[2 * nh:4 * nh, t0:t0 + c] = jax.nn.sigmoid(ab_ref[2 * nh:4 * nh, t0:t0 + c])


def _gates(ab_t, alog_b, dtb_b):
    rows, t = ab_t.shape
    tl = _tile(t, 1408, 256)
    return pl.pallas_call(
        _gates_kernel,
        grid=(t // tl,),
        in_specs=[pl.BlockSpec((rows, tl), lambda i: (0, i)),
                  pl.BlockSpec(alog_b.shape, lambda i: (0, 0)),
                  pl.BlockSpec(dtb_b.shape, lambda i: (0, 0))],
        out_specs=pl.BlockSpec((rows, tl), lambda i: (0, i)),
        out_shape=jax.ShapeDtypeStruct((rows, t), F32),
        compiler_params=_params("parallel"),
    )(ab_t, alog_b, dtb_b)


def _delta_kernel(q_ref, k_ref, v_ref, g_ref, z_ref, nw_ref, y_ref, of_ref, ob_ref, *,
                  n_lat_chunks, n_ctx_chunks, unroll, post_rows):
    c = DELTA_CHUNK
    nch = n_lat_chunks + n_ctx_chunks
    row = lax.broadcasted_iota(jnp.int32, (c, c), 0)
    col = lax.broadcasted_iota(jnp.int32, (c, c), 1)
    eye = row == col
    eye_f = eye.astype(F32)
    n_levels = int(math.log2(c))
    same_blk = [(row >> s) == (col >> s) for s in range(n_levels + 1)]
    nt = (((1,), (1,)), ((), ()))

    def to_col(r):
        return jnp.sum(jnp.where(eye, jnp.broadcast_to(r, (c, c)), 0.0), axis=1, keepdims=True)

    def chunk(idx, state, fwd):
        r0 = pl.multiple_of(idx * c, c)
        q = q_ref[pl.ds(r0, c), :]
        k = k_ref[pl.ds(r0, c), :]
        v = v_ref[pl.ds(r0, c), :]
        rows = g_ref[0, idx]
        if fwd:
            g_r, b_r = rows[0:1], rows[2:3]
            g_last = g_r[:, c - 1:c]
            mask, strict = row >= col, row > col
        else:
            g_r, b_r = rows[1:2], rows[3:4]
            g_last = g_r[:, 0:1]
            mask, strict = row <= col, row < col
        g_c = to_col(g_r)
        b_c = to_col(b_r)
        decay = jnp.where(mask, jnp.exp(jnp.where(mask, g_c - g_r, 0.0)), 0.0)
        kf = k.astype(F32)
        kb = kf * b_c
        kk = lax.dot_general(kb.astype(BF16), k, nt, preferred_element_type=F32)
        n_mat = jnp.where(strict, kk * decay, 0.0)
        x = eye_f - jnp.where(same_blk[1], n_mat, 0.0)
        for lv in range(1, n_levels):
            e = jnp.where(jnp.logical_and(same_blk[lv + 1], jnp.logical_not(same_blk[lv])), n_mat, 0.0)
            xb = x.astype(BF16)
            a = jnp.dot(xb, e.astype(BF16), preferred_element_type=F32)
            x = x - jnp.dot(a.astype(BF16), xb, preferred_element_type=F32)
        e_g = jnp.exp(g_c)
        rhs = jnp.concatenate([(v.astype(F32) * b_c).astype(BF16), (kb * e_g).astype(BF16)], axis=1)
        sol = jnp.dot(x.astype(BF16), rhs, preferred_element_type=F32)
        u = sol[:, 0:B_DV]
        w = sol[:, B_DV:]
        attn = jnp.where(mask, lax.dot_general(q, k, nt, preferred_element_type=F32) * decay, 0.0)
        qd = (q.astype(F32) * e_g).astype(BF16)
        ke = kf * jnp.exp(g_last - g_c)
        res = jnp.dot(jnp.concatenate([w.astype(BF16), qd], axis=0), state.astype(BF16),
                      preferred_element_type=F32)
        v_new = (u - res[0:c]).astype(BF16)
        o = res[c:] + jnp.dot(attn.astype(BF16), v_new, preferred_element_type=F32)
        state = state * jnp.exp(g_last) + jnp.dot(ke.T.astype(BF16), v_new, preferred_element_type=F32)
        return r0, o, state

    def body(it, carry):
        s_f, s_b = carry
        for uu in range(unroll):
            s = it * unroll + uu
            f = jnp.where(s < n_ctx_chunks, s + n_lat_chunks, s - n_ctx_chunks)
            b = nch - 1 - s
            rf, o_f, s_f = chunk(f, s_f, True)
            rb, o_b, s_b = chunk(b, s_b, False)
            of_ref[pl.ds(rf, c), :] = o_f
            ob_ref[pl.ds(rb, c), :] = o_b
        return s_f, s_b

    zero = jnp.zeros((B_DK, B_DV), F32)
    lax.fori_loop(0, nch // unroll, body, (zero, zero))

    nw = nw_ref[...]
    for r in range(0, nch * c, post_rows):
        o = of_ref[r:r + post_rows, :] + ob_ref[r:r + post_rows, :]
        ms = jnp.mean(o * o, axis=-1, keepdims=True)
        y = (o * lax.rsqrt(ms + NORM_EPS) * nw) * _silu(z_ref[r:r + post_rows, :].astype(F32))
        y_ref[r:r + post_rows, :] = y.astype(y_ref.dtype)


def _delta(qkv, g4, vz, dn_norm_w, n_lat, n_ctx):
    t = qkv.shape[0]
    c = DELTA_CHUNK
    nch = t // c
    kern = functools.partial(_delta_kernel, n_lat_chunks=n_lat // c, n_ctx_chunks=n_ctx // c,
                             unroll=2, post_rows=_tile(t, 1056, 256))
    return pl.pallas_call(
        kern,
        grid=(B_HEADS,),
        in_specs=[pl.BlockSpec((t, LANES), lambda h: (0, h)),
                  pl.BlockSpec((t, LANES), lambda h: (0, B_HEADS + h)),
                  pl.BlockSpec((t, LANES), lambda h: (0, 2 * B_HEADS + h)),
                  pl.BlockSpec((1, nch, 8, c), lambda h: (h, 0, 0, 0)),
                  pl.BlockSpec((t, LANES), lambda h: (0, 2 * A_HEADS + h)),
                  pl.BlockSpec((1, B_DV), lambda h: (0, 0))],
        out_specs=pl.BlockSpec((t, LANES), lambda h: (0, h)),
        out_shape=jax.ShapeDtypeStruct((t, V_B), BF16),
        scratch_shapes=[pltpu.VMEM((t, B_DV), F32), pltpu.VMEM((t, B_DV), F32)],
        compiler_params=_params("parallel"),
    )(qkv, qkv, qkv, g4, vz, dn_norm_w)


def _out_kernel(ya_ref, yb_ref, wa_ref, wb_ref, x_ref, mod_ref, o_ref, *, n_lat_tiles):
    is_ctx = pl.program_id(0) >= n_lat_tiles
    gate = jnp.where(is_ctx, mod_ref[1:2, :], mod_ref[0:1, :])
    acc = jnp.dot(ya_ref[...], wa_ref[...], preferred_element_type=F32)
    acc = acc + jnp.dot(yb_ref[...], wb_ref[...], preferred_element_type=F32)
    o_ref[...] = x_ref[...] + gate * acc


def _out_final_kernel(ya_ref, yb_ref, wa_ref, wb_ref, x_ref, mod_ref, fw_ref, o_ref):
    acc = jnp.dot(ya_ref[...], wa_ref[...], preferred_element_type=F32)
    acc = acc + jnp.dot(yb_ref[...], wb_ref[...], preferred_element_type=F32)
    x = x_ref[...] + mod_ref[0:1, :] * acc
    ms = jnp.mean(x * x, axis=-1, keepdims=True)
    o_ref[...] = x * lax.rsqrt(ms + NORM_EPS) * fw_ref[...]


def _out_proj(ya, yb, wa, wb, xcat, mod, n_rows, n_lat, final_w=None):
    d = xcat.shape[1]
    tm = 256
    in_specs = [pl.BlockSpec((tm, V_A), lambda i: (i, 0)),
                pl.BlockSpec((tm, V_B), lambda i: (i, 0)),
                pl.BlockSpec((V_A, d), lambda i: (0, 0)),
                pl.BlockSpec((V_B, d), lambda i: (0, 0)),
                pl.BlockSpec((tm, d), lambda i: (i, 0)),
                pl.BlockSpec((8, d), lambda i: (0, 2))]
    args = [ya, yb, wa, wb, xcat, mod]
    if final_w is None:
        kern = functools.partial(_out_kernel, n_lat_tiles=n_lat // tm)
    else:
        kern = _out_final_kernel
        in_specs.append(pl.BlockSpec((1, d), lambda i: (0, 0)))
        args.append(final_w)
    return pl.pallas_call(
        kern,
        grid=(n_rows // tm,),
        in_specs=in_specs,
        out_specs=pl.BlockSpec((tm, d), lambda i: (i, 0)),
        out_shape=jax.ShapeDtypeStruct((n_rows, d), F32),
        compiler_params=_params("parallel"),
    )(*args)


def _rope_tables(n_lat, n_ctx):
    n_rows = n_lat // GRID_W
    rows = jnp.repeat(jnp.arange(n_rows, dtype=F32), GRID_W)
    cols = jnp.tile(jnp.arange(GRID_W, dtype=F32), n_rows)
    n_freq = A_HEAD_DIM // 4
    inv = ROPE_THETA ** (-jnp.arange(n_freq, dtype=F32) / n_freq)
    ang = jnp.concatenate([rows[:, None] * inv, cols[:, None] * inv], axis=-1)
    cos = jnp.cos(ang)
    sin = jnp.sin(ang)
    cos_t = jnp.tile(cos, (1, LANES // (A_HEAD_DIM // 2)))
    sin_t = jnp.tile(jnp.concatenate([-sin, sin], axis=-1), (1, LANES // A_HEAD_DIM))
    cos_t = jnp.concatenate([cos_t, jnp.ones((n_ctx, LANES), F32)], axis=0)
    sin_t = jnp.concatenate([sin_t, jnp.zeros((n_ctx, LANES), F32)], axis=0)
    return cos_t, sin_t


def kernel(x, c, ctx, c_ctx, w_ada, b_ada, norm_w, w_in, conv_w, dn_a_log, dn_dt_bias, dn_norm_w,
           diff_lambda, subln_w, w_out, final_norm_w):
    depth = w_ada.shape[0]
    n_lat = x.shape[1]
    n_ctx = ctx.shape[1]
    t = n_lat + n_ctx
    d = x.shape[2]
    assert x.shape[0] == 1 and d == D_MODEL
    assert n_lat % 512 == 0 and n_ctx == 256 and t % 768 == 0

    xcat = jnp.concatenate([x[0], ctx[0]], axis=0)
    cc = jnp.zeros((8, d), F32).at[0].set(c[0]).at[1].set(c_ctx)
    mods = _ada_all(cc, w_ada, b_ada)
    cos_t, sin_t = _rope_tables(n_lat, n_ctx)

    o_q, o_k, o_v, o_z, o_b, o_zb, o_ab = 0, QK_A, 2 * QK_A, 2 * QK_A + V_A, 2 * QK_A + 2 * V_A, \
        2 * QK_A + 2 * V_A + CONV_CH, 2 * QK_A + 2 * V_A + CONV_CH + V_B
    w_qk = jnp.concatenate([w_in[:, :, o_q:o_k] * (A_HEAD_DIM ** -0.5), w_in[:, :, o_k:o_v]], axis=-1).astype(BF16)
    w_vz = jnp.concatenate([w_in[:, :, o_v:o_b], w_in[:, :, o_zb:o_ab]], axis=-1).astype(BF16)
    w_b = jnp.concatenate([w_in[:, :, o_b:o_zb], w_in[:, :, o_ab:],
                           jnp.zeros((depth, d, LANES - 4 * B_HEADS), F32)], axis=-1).astype(BF16)
    w_oa = w_out[:, :V_A, :].astype(BF16)
    w_ob = w_out[:, V_A:, :].astype(BF16)

    nch = t // DELTA_CHUNK
    for i in range(depth):
        last = i == depth - 1
        lam_init = 0.8 - 0.6 * math.exp(-0.3 * i)
        mod = mods[i]
        xn = _norm_mod(xcat, norm_w[i][None, :], mod, n_lat)
        qk = _proj(xn, w_qk[i], BF16, 256, rope=(cos_t, sin_t))
        vz = _proj(xn, w_vz[i], BF16, 512)
        pb = _proj(xn, w_b[i], F32, 640)

        vt = vz[:, :V_A].reshape(t, A_HEADS, A_VAL_DIM).transpose(1, 2, 0)
        vt = jnp.concatenate([vt, jnp.ones((A_HEADS, 1, t), BF16),
                              jnp.zeros((A_HEADS, V_ROWS - A_VAL_DIM - 1, t), BF16)], axis=1)
        lam_p = diff_lambda[i]
        sw = subln_w[i][None, :]
        ya = _attention(qk, vz, vt, lam_p, sw, lam_init, q_row0=0, n_q=n_lat, k_row0=0, n_keys=t,
                        tq=512, tk=768)
        if not last:
            ya = _attention(qk, vz, vt, lam_p, sw, lam_init, q_row0=n_lat, n_q=n_ctx, k_row0=n_lat,
                            n_keys=n_ctx, tq=256, tk=256, ya=ya)

        qkv = _conv_prep(pb, conv_w[i], n_lat, n_ctx)
        ab_t = pb[:, CONV_CH:CONV_CH + 4 * B_HEADS].T
        alog_b = jnp.broadcast_to(dn_a_log[i].reshape(2 * B_HEADS, 1), (2 * B_HEADS, DELTA_CHUNK))
        dtb_b = jnp.broadcast_to(dn_dt_bias[i].reshape(2 * B_HEADS, 1), (2 * B_HEADS, DELTA_CHUNK))
        gates = _gates(ab_t, alog_b, dtb_b)
        g4 = gates.reshape(4, B_HEADS, nch, DELTA_CHUNK).transpose(1, 2, 0, 3)
        g4 = jnp.concatenate([g4, jnp.zeros_like(g4)], axis=2)
        yb = _delta(qkv, g4, vz, dn_norm_w[i][None, :], n_lat, n_ctx)

        if not last:
            xcat = _out_proj(ya, yb, w_oa[i], w_ob[i], xcat, mod, t, n_lat)
        else:
            out = _out_proj(ya, yb, w_oa[i], w_ob[i], xcat, mod, n_lat, n_lat, final_w=final_norm_w[None, :])
    return out[None]
```

```python
import functools
import math

import jax
import jax.numpy as jnp
from jax import lax
from jax.experimental import pallas as pl
from jax.experimental.pallas import tpu as pltpu

F32 = jnp.float32
BF16 = jnp.bfloat16

LANES = 128
VMEM_LIMIT_BYTES = 56 * 1024 * 1024

D_MODEL = 2048
GRID_W = 64
A_HEADS = 8
A_HEAD_DIM = 64
A_VAL_DIM = 2 * A_HEAD_DIM
QK_A = A_HEADS * 2 * A_HEAD_DIM
V_A = A_HEADS * A_VAL_DIM
B_HEADS = 8
B_DK = 128
B_DV = 128
QK_B = B_HEADS * B_DK
V_B = B_HEADS * B_DV
CONV_K = 5
CONV_CH = 2 * QK_B + V_B
ROPE_THETA = 10000.0
NORM_EPS = 1e-6
SUBLN_EPS = 1e-5
L2_EPS = 1e-6

DELTA_CHUNK = 128
V_ROWS = A_VAL_DIM + 16
NEG_BIG = -1e30


def _params(*sem, flags=None):
    return pltpu.CompilerParams(dimension_semantics=sem, vmem_limit_bytes=VMEM_LIMIT_BYTES, flags=flags)


def _silu(x):
    return x * jax.nn.sigmoid(x)


def _tile(n, preferred, fallback):
    return preferred if n % preferred == 0 else fallback


def _ada_kernel(c_ref, w_ref, b_ref, o_ref):
    s = _silu(c_ref[...])
    o_ref[0] = jnp.dot(s.astype(BF16), w_ref[0].astype(BF16), preferred_element_type=F32) + b_ref[0]


def _ada_all(cc, w_ada, b_ada):
    depth, d, n = w_ada.shape
    tn = 512
    return pl.pallas_call(
        _ada_kernel,
        grid=(depth, n // tn),
        in_specs=[pl.BlockSpec((8, d), lambda l, j: (0, 0)),
                  pl.BlockSpec((1, d, tn), lambda l, j: (l, 0, j)),
                  pl.BlockSpec((1, 1, tn), lambda l, j: (l, 0, j))],
        out_specs=pl.BlockSpec((1, 8, tn), lambda l, j: (l, 0, j)),
        out_shape=jax.ShapeDtypeStruct((depth, 8, n), F32),
        compiler_params=_params("parallel", "parallel"),
    )(cc, w_ada, b_ada.reshape(depth, 1, n))


def _modulated_norm(x, nw, shift, scale):
    ms = jnp.mean(x * x, axis=-1, keepdims=True)
    return (x * lax.rsqrt(ms + NORM_EPS) * nw) * (1.0 + scale) + shift


def _norm_kernel(x_ref, nw_ref, mod_ref, o_ref, *, n_lat_tiles):
    is_ctx = pl.program_id(0) >= n_lat_tiles
    d = x_ref.shape[1]
    shift = jnp.where(is_ctx, mod_ref[1:2, 0:d], mod_ref[0:1, 0:d])
    scale = jnp.where(is_ctx, mod_ref[1:2, d:2 * d], mod_ref[0:1, d:2 * d])
    o_ref[...] = _modulated_norm(x_ref[...], nw_ref[...], shift, scale).astype(o_ref.dtype)


def _norm_mod(xcat, nw, mod, n_lat):
    t, d = xcat.shape
    tm = 256
    return pl.pallas_call(
        functools.partial(_norm_kernel, n_lat_tiles=n_lat // tm),
        grid=(t // tm,),
        in_specs=[pl.BlockSpec((tm, d), lambda i: (i, 0)),
                  pl.BlockSpec((1, d), lambda i: (0, 0)),
                  pl.BlockSpec((8, 2 * d), lambda i: (0, 0))],
        out_specs=pl.BlockSpec((tm, d), lambda i: (i, 0)),
        out_shape=jax.ShapeDtypeStruct((t, d), BF16),
        compiler_params=_params("parallel"),
    )(xcat, nw, mod)


def _mm_kernel(x_ref, w_ref, o_ref):
    o_ref[...] = jnp.dot(x_ref[...], w_ref[0], preferred_element_type=F32).astype(o_ref.dtype)


def _mm_t_kernel(x_ref, w_ref, o_ref, acc_ref):
    acc_ref[...] = jnp.dot(x_ref[...], w_ref[0], preferred_element_type=F32)
    o_ref[...] = acc_ref[...].T.astype(o_ref.dtype)


def _mm_rope_kernel(x_ref, w_ref, cos_ref, sin_ref, o_ref, *, q_tiles):
    scale = jnp.where(pl.program_id(1) < q_tiles, A_HEAD_DIM ** -0.5, 1.0)
    t = jnp.dot(x_ref[...], w_ref[0], preferred_element_type=F32) * scale
    c = cos_ref[...]
    s = sin_ref[...]
    lane = lax.broadcasted_iota(jnp.int32, c.shape, 1)
    first = (lane % A_HEAD_DIM) < (A_HEAD_DIM // 2)
    for a in range(0, t.shape[1], LANES):
        th = t[:, a:a + LANES]
        sw = jnp.where(first, pltpu.roll(th, LANES - A_HEAD_DIM // 2, 1), pltpu.roll(th, A_HEAD_DIM // 2, 1))
        o_ref[:, a:a + LANES] = (th * c + sw * s).astype(o_ref.dtype)


def _proj(xn, w, layer, col_segments, out_dtype, tn, rope_q_cols=None, rope=None, transposed=False):
    t, d = xn.shape
    tm = _tile(t, 1408, 256)
    assert all(c0 % tn == 0 and width % tn == 0 for c0, width in col_segments)
    n_tiles = sum(width // tn for _, width in col_segments)
    scratch = []
    bounds, offs, acc = [], [], 0
    for c0, width in col_segments:
        offs.append(c0 // tn - acc)
        acc += width // tn
        bounds.append(acc)

    def wcol(j):
        col = j + offs[-1]
        for bnd, off in zip(reversed(bounds[:-1]), reversed(offs[:-1])):
            col = jnp.where(j < bnd, j + off, col)
        return col

    in_specs = [pl.BlockSpec((tm, d), lambda i, j: (i, 0)),
                pl.BlockSpec((1, d, tn), lambda i, j: (layer, 0, wcol(j)))]
    args = [xn, w]
    kern = _mm_kernel
    out_spec = pl.BlockSpec((tm, tn), lambda i, j: (i, j))
    out_shape = (t, n_tiles * tn)
    if rope is not None:
        in_specs += [pl.BlockSpec((tm, LANES), lambda i, j: (i, 0))] * 2
        args += list(rope)
        kern = functools.partial(_mm_rope_kernel, q_tiles=rope_q_cols // tn)
    if transposed:
        kern = _mm_t_kernel
        out_spec = pl.BlockSpec((tn, tm), lambda i, j: (j, i))
        out_shape = (n_tiles * tn, t)
        scratch = [pltpu.VMEM((tm, tn), F32)]
    return pl.pallas_call(
        kern,
        grid=(t // tm, n_tiles),
        in_specs=in_specs,
        out_specs=out_spec,
        out_shape=jax.ShapeDtypeStruct(out_shape, out_dtype),
        scratch_shapes=scratch,
        compiler_params=_params("parallel", "parallel"),
    )(*args)


def _attn_kernel(q_ref, k_ref, vt_ref, z_ref, lam_ref, sw_ref, o_ref, *, tk, lam_init):
    tq = q_ref.shape[0]
    nkeys = k_ref.shape[0]
    q = q_ref[...]
    lane = lax.broadcasted_iota(jnp.int32, q.shape, 1)
    zero = jnp.zeros_like(q)
    qz = jnp.concatenate([jnp.where(lane < A_HEAD_DIM, q, zero),
                          jnp.where(lane >= A_HEAD_DIM, q, zero)], axis=0)
    m = jnp.full((1, 2 * tq), NEG_BIG, F32)
    acc = jnp.zeros((V_ROWS, 2 * tq), F32)
    tail_row = lax.broadcasted_iota(jnp.int32, (V_ROWS - A_VAL_DIM, tk), 0)
    tail = jnp.where(tail_row == 0, 1.0, 0.0).astype(BF16)

    def scores(c0):
        return lax.dot_general(k_ref[c0:c0 + tk, :], qz, (((1,), (1,)), ((), ())),
                               preferred_element_type=F32)

    st_next = scores(0)
    for c0 in range(0, nkeys, tk):
        st = st_next
        if c0 + tk < nkeys:
            st_next = scores(c0 + tk)
        m_new = jnp.maximum(m, jnp.max(st, axis=0, keepdims=True))
        alpha = jnp.exp(m - m_new)
        p = jnp.exp(st - m_new).astype(BF16)
        v_ext = jnp.concatenate([vt_ref[:, c0:c0 + tk], tail], axis=0)
        acc = alpha * acc + jnp.dot(v_ext, p, preferred_element_type=F32)
        m = m_new
    o1 = acc[0:A_VAL_DIM, 0:tq] / acc[A_VAL_DIM:A_VAL_DIM + 1, 0:tq]
    o2 = acc[0:A_VAL_DIM, tq:] / acc[A_VAL_DIM:A_VAL_DIM + 1, tq:]
    lp = lam_ref[...]
    lam = (jnp.exp(jnp.sum(lp[0:1] * lp[1:2], axis=1, keepdims=True))
           - jnp.exp(jnp.sum(lp[2:3] * lp[3:4], axis=1, keepdims=True)) + lam_init)
    o = (o1 - lam * o2).T
    ms = jnp.mean(o * o, axis=-1, keepdims=True)
    y = (o * lax.rsqrt(ms + SUBLN_EPS) * sw_ref[...]) * (1.0 - lam_init)
    o_ref[...] = (y * _silu(z_ref[...].astype(F32))).astype(o_ref.dtype)


def _attention(qk, zz, vt, lam_p, subln_w, lam_init, *, q_row0, n_q, k_row0, n_keys, tq, tk):
    qb0 = q_row0 // tq
    kb0 = k_row0 // n_keys
    return pl.pallas_call(
        functools.partial(_attn_kernel, tk=tk, lam_init=lam_init),
        grid=(A_HEADS, n_q // tq),
        in_specs=[pl.BlockSpec((tq, LANES), lambda h, i: (qb0 + i, h)),
                  pl.BlockSpec((n_keys, LANES), lambda h, i: (kb0, A_HEADS + h)),
                  pl.BlockSpec((A_VAL_DIM, n_keys), lambda h, i: (h, kb0)),
                  pl.BlockSpec((tq, LANES), lambda h, i: (qb0 + i, h)),
                  pl.BlockSpec(lam_p.shape, lambda h, i: (0, 0)),
                  pl.BlockSpec((1, A_VAL_DIM), lambda h, i: (0, 0))],
        out_specs=pl.BlockSpec((tq, LANES), lambda h, i: (i, h)),
        out_shape=jax.ShapeDtypeStruct((n_q, V_A), BF16),
        compiler_params=_params("parallel", "parallel"),
    )(qk, qk, vt, zz, lam_p, subln_w)


def _conv_kernel(x_ref, w_ref, o_ref, pad_ref, *, n_lat, n_ctx, rows):
    j = pl.program_id(0)
    half = (CONV_K - 1) // 2
    lat0 = 8
    ctx0 = lat0 + n_lat + 16
    zeros8 = jnp.zeros((8, LANES), F32)
    pad_ref[0:8, :] = zeros8
    pad_ref[lat0 + n_lat:lat0 + n_lat + 8, :] = zeros8
    pad_ref[lat0 + n_lat + 8:ctx0, :] = zeros8
    pad_ref[ctx0 + n_ctx:ctx0 + n_ctx + 8, :] = zeros8
    pad_ref[lat0:lat0 + n_lat, :] = x_ref[0:n_lat, :]
    pad_ref[ctx0:ctx0 + n_ctx, :] = x_ref[n_lat:n_lat + n_ctx, :]
    w = w_ref[...]
    is_qk = j < 2 * B_HEADS
    post = jnp.where(j < B_HEADS, B_DK ** -0.5, 1.0)
    for src0, dst0, n in ((lat0, 0, n_lat), (ctx0, n_lat, n_ctx)):
        for r in range(0, n, rows):
            acc = jnp.zeros((rows, LANES), F32)
            for tap in range(CONV_K):
                a = src0 + r + tap - half
                acc = acc + pad_ref[a:a + rows, :] * w[tap:tap + 1, :]
            y = _silu(acc)
            nrm = lax.rsqrt(jnp.sum(y * y, axis=-1, keepdims=True) + L2_EPS) * post
            y = y * jnp.where(is_qk, nrm, 1.0)
            o_ref[dst0 + r:dst0 + r + rows, :] = y.astype(o_ref.dtype)


def _conv_prep(pb, conv_w, n_lat, n_ctx):
    t = pb.shape[0]
    nblk = CONV_CH // LANES
    return pl.pallas_call(
        functools.partial(_conv_kernel, n_lat=n_lat, n_ctx=n_ctx, rows=256),
        grid=(nblk,),
        in_specs=[pl.BlockSpec((t, LANES), lambda j: (0, j)),
                  pl.BlockSpec((CONV_K, LANES), lambda j: (0, j))],
        out_specs=pl.BlockSpec((t, LANES), lambda j: (0, j)),
        out_shape=jax.ShapeDtypeStruct((t, CONV_CH), BF16),
        scratch_shapes=[pltpu.VMEM((t + 40, LANES), F32)],
        compiler_params=_params("parallel"),
    )(pb, conv_w)


def _gates_kernel(ab_ref, alog_ref, dtb_ref, o_ref):
    c = DELTA_CHUNK
    nh = B_HEADS
    r = lax.broadcasted_iota(jnp.int32, (c, c), 0)
    l = lax.broadcasted_iota(jnp.int32, (c, c), 1)
    upper = (r <= l).astype(F32)
    lower = (r >= l).astype(F32)
    neg_a = -jnp.exp(alog_ref[...])
    dtb = dtb_ref[...]
    for t0 in range(0, ab_ref.shape[1], c):
        x = ab_ref[0:2 * nh, t0:t0 + c] + dtb
        sp = jnp.maximum(x, 0.0) + jnp.log1p(jnp.exp(-jnp.abs(x)))
        g = neg_a * sp
        o_ref[0:nh, t0:t0 + c] = jnp.dot(g[0:nh], upper, preferred_element_type=F32,
                                          precision=lax.Precision.HIGHEST)
        o_ref[nh:2 * nh, t0:t0 + c] = jnp.dot(g[nh:2 * nh], lower, preferred_element_type=F32,
                                               precision=lax.Precision.HIGHEST)
        o_ref[2 * nh:4 * nh, t0:t0 + c] = jax.nn.sigmoid(ab_ref[2 * nh:4 * nh, t0:t0 + c])


def _gates(ab_t, alog_b, dtb_b):
    rows, t = ab_t.shape
    tl = _tile(t, 1408, 256)
    return pl.pallas_call(
        _gates_kernel,
        grid=(t // tl,),
        in_specs=[pl.BlockSpec((rows, tl), lambda i: (0, i)),
                  pl.BlockSpec(alog_b.shape, lambda i: (0, 0)),
                  pl.BlockSpec(dtb_b.shape, lambda i: (0, 0))],
        out_specs=pl.BlockSpec((rows, tl), lambda i: (0, i)),
        out_shape=jax.ShapeDtypeStruct((rows, t), F32),
        compiler_params=_params("parallel"),
    )(ab_t, alog_b, dtb_b)


def _delta_prep_kernel(q_ref, k_ref, v_ref, g_ref, aq_ref, bo_ref, e_ref, *, n_chunks):
    c = DELTA_CHUNK
    row = lax.broadcasted_iota(jnp.int32, (c, c), 0)
    col = lax.broadcasted_iota(jnp.int32, (c, c), 1)
    eye = row == col
    eye_f = eye.astype(F32)
    n_levels = int(math.log2(c))
    same_blk = [(row >> s) == (col >> s) for s in range(n_levels + 1)]
    off_blk = [None] + [jnp.logical_and(same_blk[lv + 1], jnp.logical_not(same_blk[lv]))
                        for lv in range(1, n_levels)]
    nt = (((1,), (1,)), ((), ()))

    def to_col(r):
        return jnp.sum(jnp.where(eye, jnp.broadcast_to(r, (c, c)), 0.0), axis=1, keepdims=True)

    chains = []
    for ci in range(n_chunks):
        q = q_ref[ci * c:(ci + 1) * c, :]
        k = k_ref[ci * c:(ci + 1) * c, :]
        v = v_ref[ci * c:(ci + 1) * c, :]
        rows = g_ref[0, ci]
        kk = lax.dot_general(k, k, nt, preferred_element_type=F32)
        qk = lax.dot_general(q, k, nt, preferred_element_type=F32)
        for d in range(2):
            if d == 0:
                g_r, b_r = rows[0:1], rows[2:3]
                g_last = g_r[:, c - 1:c]
                mask, strict = row >= col, row > col
            else:
                g_r, b_r = rows[1:2], rows[3:4]
                g_last = g_r[:, 0:1]
                mask, strict = row <= col, row < col
            g_c = to_col(g_r)
            b_c = to_col(b_r)
            decay = jnp.where(mask, jnp.exp(jnp.where(mask, g_c - g_r, 0.0)), 0.0)
            n_mat = jnp.where(strict, kk * b_c * decay, 0.0)
            chains.append(dict(ci=ci, d=d, q=q, k=k, v=v, qk=qk, g_c=g_c, b_c=b_c, g_last=g_last, mask=mask,
                               decay=decay, n_mat=n_mat, x=eye_f - jnp.where(same_blk[1], n_mat, 0.0)))
    for lv in range(1, n_levels):
        for ch in chains:
            ch["xb"] = ch["x"].astype(BF16)
            ch["a"] = jnp.dot(ch["xb"], jnp.where(off_blk[lv], ch["n_mat"], 0.0).astype(BF16),
                              preferred_element_type=F32)
        for ch in chains:
            ch["x"] = ch["x"] - jnp.dot(ch["a"].astype(BF16), ch["xb"], preferred_element_type=F32)
    for ch in chains:
        kf = ch["k"].astype(F32)
        ch["e_g"] = jnp.exp(ch["g_c"])
        kb = kf * ch["b_c"]
        rhs = jnp.concatenate([(kb * ch["e_g"]).astype(BF16),
                               (ch["v"].astype(F32) * ch["b_c"]).astype(BF16)], axis=1)
        ch["wu"] = jnp.dot(ch["x"].astype(BF16), rhs, preferred_element_type=F32).astype(BF16)
        ch["attn"] = jnp.where(ch["mask"], ch["qk"] * ch["decay"], 0.0).astype(BF16)
        ch["ke_t"] = (kf * jnp.exp(ch["g_last"] - ch["g_c"])).T.astype(BF16)
    for ch in chains:
        ch["m1"] = jnp.dot(ch["ke_t"], ch["wu"], preferred_element_type=F32)
        ch["m2"] = jnp.dot(ch["attn"], ch["wu"], preferred_element_type=F32)
    for ch in chains:
        ci, d, m1, m2 = ch["ci"], ch["d"], ch["m1"], ch["m2"]
        aq_ref[d, 0, ci, 0:c, :] = (-m1[:, 0:B_DV]).astype(BF16)
        aq_ref[d, 0, ci, c:2 * c, :] = (ch["q"].astype(F32) * ch["e_g"] - m2[:, 0:B_DV]).astype(BF16)
        bo_ref[d, 0, ci, 0:c, :] = m1[:, B_DV:]
        bo_ref[d, 0, ci, c:2 * c, :] = m2[:, B_DV:]
    for ci in range(n_chunks):
        e_f, e_b = (jnp.broadcast_to(jnp.exp(chains[2 * ci + d]["g_last"]), (1, c)) for d in range(2))
        e_ref[0, ci] = jnp.concatenate([e_f, e_b, jnp.zeros((6, c), F32)], axis=0)


def _delta_prep(qkv, g4):
    t = qkv.shape[0]
    c = DELTA_CHUNK
    nch = t // c
    g = 6
    rows = g * c
    return pl.pallas_call(
        functools.partial(_delta_prep_kernel, n_chunks=g),
        grid=(B_HEADS, nch // g),
        in_specs=[pl.BlockSpec((rows, LANES), lambda h, i: (i, h)),
                  pl.BlockSpec((rows, LANES), lambda h, i: (i, B_HEADS + h)),
                  pl.BlockSpec((rows, LANES), lambda h, i: (i, 2 * B_HEADS + h)),
                  pl.BlockSpec((1, g, 8, c), lambda h, i: (h, i, 0, 0))],
        out_specs=[pl.BlockSpec((2, 1, g, 2 * c, B_DV), lambda h, i: (0, h, i, 0, 0)),
                   pl.BlockSpec((2, 1, g, 2 * c, B_DV), lambda h, i: (0, h, i, 0, 0)),
                   pl.BlockSpec((1, g, 8, c), lambda h, i: (h, i, 0, 0))],
        out_shape=[jax.ShapeDtypeStruct((2, B_HEADS, nch, 2 * c, B_DV), BF16),
                   jax.ShapeDtypeStruct((2, B_HEADS, nch, 2 * c, B_DV), F32),
                   jax.ShapeDtypeStruct((B_HEADS, nch, 8, c), F32)],
        compiler_params=_params("parallel", "parallel"),
    )(qkv, qkv, qkv, g4)


def _delta_scan_kernel(aqf_ref, bof_ref, ef_ref, aqb_ref, bob_ref, eb_ref, of_ref, ob_ref, st_ref):
    c = DELTA_CHUNK

    @pl.when(pl.program_id(0) == 0)
    def _():
        st_ref[...] = jnp.zeros_like(st_ref)

    for d, (aq_ref, bo_ref, e_ref, o_ref) in enumerate(((aqf_ref, bof_ref, ef_ref, of_ref),
                                                        (aqb_ref, bob_ref, eb_ref, ob_ref))):
        for h in range(B_HEADS):
            state = st_ref[d, h]
            res = jnp.dot(aq_ref[0, h, 0], state.astype(BF16), preferred_element_type=F32)
            bo = bo_ref[0, h, 0]
            o_ref[:, h * B_DV:(h + 1) * B_DV] = res[c:] + bo[c:]
            st_ref[d, h] = state * e_ref[h, 0, d:d + 1, 0:1] + res[0:c] + bo[0:c]


def _delta_scan(aq, bo, ee, n_lat, n_ctx):
    c = DELTA_CHUNK
    nch = aq.shape[2]
    n_lat_c = n_lat // c
    n_ctx_c = n_ctx // c
    t = nch * c

    def fwd(s):
        return jnp.where(s < n_ctx_c, s + n_lat_c, s - n_ctx_c)

    def bwd(s):
        return nch - 1 - s

    def big(direction, order):
        return pl.BlockSpec((1, B_HEADS, 1, 2 * c, B_DV), lambda s: (direction, 0, order(s), 0, 0))

    def small(order):
        return pl.BlockSpec((B_HEADS, 1, 8, c), lambda s: (0, order(s), 0, 0))

    return pl.pallas_call(
        _delta_scan_kernel,
        grid=(nch,),
        in_specs=[big(0, fwd), big(0, fwd), small(fwd), big(1, bwd), big(1, bwd), small(bwd)],
        out_specs=[pl.BlockSpec((c, V_B), lambda s: (fwd(s), 0)),
                   pl.BlockSpec((c, V_B), lambda s: (bwd(s), 0))],
        out_shape=[jax.ShapeDtypeStruct((t, V_B), F32), jax.ShapeDtypeStruct((t, V_B), F32)],
        scratch_shapes=[pltpu.VMEM((2, B_HEADS, B_DK, B_DV), F32)],
        compiler_params=_params("arbitrary"),
    )(aq, bo, ee, aq, bo, ee)


def _delta_post_kernel(of_ref, ob_ref, z_ref, nw_ref, y_ref):
    nw = nw_ref[...]
    for h in range(B_HEADS):
        sl = slice(h * B_DV, (h + 1) * B_DV)
        o = of_ref[:, sl] + ob_ref[:, sl]
        ms = jnp.mean(o * o, axis=-1, keepdims=True)
        y = (o * lax.rsqrt(ms + NORM_EPS) * nw) * _silu(z_ref[:, sl].astype(F32))
        y_ref[:, sl] = y.astype(y_ref.dtype)


def _delta_post(o_f, o_b, zz, dn_norm_w):
    t = o_f.shape[0]
    tm = 256
    zb_blk = V_A // V_B
    return pl.pallas_call(
        _delta_post_kernel,
        grid=(t // tm,),
        in_specs=[pl.BlockSpec((tm, V_B), lambda i: (i, 0)),
                  pl.BlockSpec((tm, V_B), lambda i: (i, 0)),
                  pl.BlockSpec((tm, V_B), lambda i: (i, zb_blk)),
                  pl.BlockSpec((1, B_DV), lambda i: (0, 0))],
        out_specs=pl.BlockSpec((tm, V_B), lambda i: (i, 0)),
        out_shape=jax.ShapeDtypeStruct((t, V_B), BF16),
        compiler_params=_params("parallel"),
    )(o_f, o_b, zz, dn_norm_w)


def _out_kernel(ya_ref, yac_ref, yb_ref, wa_ref, wb_ref, x_ref, mod_ref, nw_ref, modn_ref, o_ref, xn_ref, *,
                n_lat_tiles):
    is_ctx = pl.program_id(0) >= n_lat_tiles
    d = x_ref.shape[1]
    gate = jnp.where(is_ctx, mod_ref[1:2, :], mod_ref[0:1, :])
    ya = jnp.where(is_ctx, yac_ref[...], ya_ref[...])
    acc = jnp.dot(ya, wa_ref[0], preferred_element_type=F32)
    acc = acc + jnp.dot(yb_ref[...], wb_ref[0], preferred_element_type=F32)
    x = x_ref[...] + gate * acc
    o_ref[...] = x
    shift = jnp.where(is_ctx, modn_ref[1:2, 0:d], modn_ref[0:1, 0:d])
    scale = jnp.where(is_ctx, modn_ref[1:2, d:2 * d], modn_ref[0:1, d:2 * d])
    xn_ref[...] = _modulated_norm(x, nw_ref[...], shift, scale).astype(xn_ref.dtype)


def _out_final_kernel(ya_ref, yb_ref, wa_ref, wb_ref, x_ref, mod_ref, fw_ref, o_ref):
    acc = jnp.dot(ya_ref[...], wa_ref[0], preferred_element_type=F32)
    acc = acc + jnp.dot(yb_ref[...], wb_ref[0], preferred_element_type=F32)
    x = x_ref[...] + mod_ref[0:1, :] * acc
    ms = jnp.mean(x * x, axis=-1, keepdims=True)
    o_ref[...] = x * lax.rsqrt(ms + NORM_EPS) * fw_ref[...]


def _out_proj(ya, ya_ctx, yb, w_out, layer, xcat, mod, nw_next, mod_next, n_lat):
    t, d = xcat.shape
    tm = 256
    n_lat_tiles = n_lat // tm
    return pl.pallas_call(
        functools.partial(_out_kernel, n_lat_tiles=n_lat_tiles),
        grid=(t // tm,),
        in_specs=[pl.BlockSpec((tm, V_A), lambda i: (jnp.minimum(i, n_lat_tiles - 1), 0)),
                  pl.BlockSpec((tm, V_A), lambda i: (jnp.maximum(i - n_lat_tiles, 0), 0)),
                  pl.BlockSpec((tm, V_B), lambda i: (i, 0)),
                  pl.BlockSpec((1, V_A, d), lambda i: (layer, 0, 0)),
                  pl.BlockSpec((1, V_B, d), lambda i: (layer, V_A // V_B, 0)),
                  pl.BlockSpec((tm, d), lambda i: (i, 0)),
                  pl.BlockSpec((8, d), lambda i: (0, 2)),
                  pl.BlockSpec((1, d), lambda i: (0, 0)),
                  pl.BlockSpec((8, 2 * d), lambda i: (0, 0))],
        out_specs=[pl.BlockSpec((tm, d), lambda i: (i, 0)),
                   pl.BlockSpec((tm, d), lambda i: (i, 0))],
        out_shape=[jax.ShapeDtypeStruct((t, d), F32), jax.ShapeDtypeStruct((t, d), BF16)],
        compiler_params=_params("parallel"),
    )(ya, ya_ctx, yb, w_out, w_out, xcat, mod, nw_next, mod_next)


def _out_proj_final(ya, yb, w_out, layer, xcat, mod, final_w, n_lat):
    d = xcat.shape[1]
    tm = 256
    return pl.pallas_call(
        _out_final_kernel,
        grid=(n_lat // tm,),
        in_specs=[pl.BlockSpec((tm, V_A), lambda i: (i, 0)),
                  pl.BlockSpec((tm, V_B), lambda i: (i, 0)),
                  pl.BlockSpec((1, V_A, d), lambda i: (layer, 0, 0)),
                  pl.BlockSpec((1, V_B, d), lambda i: (layer, V_A // V_B, 0)),
                  pl.BlockSpec((tm, d), lambda i: (i, 0)),
                  pl.BlockSpec((8, d), lambda i: (0, 2)),
                  pl.BlockSpec((1, d), lambda i: (0, 0))],
        out_specs=pl.BlockSpec((tm, d), lambda i: (i, 0)),
        out_shape=jax.ShapeDtypeStruct((n_lat, d), F32),
        compiler_params=_params("parallel"),
    )(ya, yb, w_out, w_out, xcat, mod, final_w)


def _rope_tables(n_lat, n_ctx):
    n_rows = n_lat // GRID_W
    rows = jnp.repeat(jnp.arange(n_rows, dtype=F32), GRID_W)
    cols = jnp.tile(jnp.arange(GRID_W, dtype=F32), n_rows)
    n_freq = A_HEAD_DIM // 4
    inv = ROPE_THETA ** (-jnp.arange(n_freq, dtype=F32) / n_freq)
    ang = jnp.concatenate([rows[:, None] * inv, cols[:, None] * inv], axis=-1)
    cos = jnp.cos(ang)
    sin = jnp.sin(ang)
    cos_t = jnp.tile(cos, (1, LANES // (A_HEAD_DIM // 2)))
    sin_t = jnp.tile(jnp.concatenate([-sin, sin], axis=-1), (1, LANES // A_HEAD_DIM))
    cos_t = jnp.concatenate([cos_t, jnp.ones((n_ctx, LANES), F32)], axis=0)
    sin_t = jnp.concatenate([sin_t, jnp.zeros((n_ctx, LANES), F32)], axis=0)
    return cos_t, sin_t


def kernel(x, c, ctx, c_ctx, w_ada, b_ada, norm_w, w_in, conv_w, dn_a_log, dn_dt_bias, dn_norm_w,
           diff_lambda, subln_w, w_out, final_norm_w):
    depth = w_ada.shape[0]
    n_lat = x.shape[1]
    n_ctx = ctx.shape[1]
    t = n_lat + n_ctx
    d = x.shape[2]
    assert x.shape[0] == 1 and d == D_MODEL
    assert n_lat % 512 == 0 and n_ctx == 256 and t % 768 == 0

    xcat = jnp.concatenate([x[0], ctx[0]], axis=0)
    cc = jnp.zeros((8, d), F32).at[0].set(c[0]).at[1].set(c_ctx)
    mods = _ada_all(cc, w_ada, b_ada)
    cos_t, sin_t = _rope_tables(n_lat, n_ctx)

    o_q, o_v, o_z, o_b, o_zb, o_ab = 0, 2 * QK_A, 2 * QK_A + V_A, 2 * QK_A + 2 * V_A, \
        2 * QK_A + 2 * V_A + CONV_CH, 2 * QK_A + 2 * V_A + CONV_CH + V_B
    w_in_bf = w_in.astype(BF16)
    w_out_bf = w_out.astype(BF16)
    w_ab = jnp.pad(w_in[:, :, o_ab:], ((0, 0), (0, 0), (0, LANES - 4 * B_HEADS))).astype(BF16)

    nch = t // DELTA_CHUNK
    xn = _norm_mod(xcat, norm_w[0][None, :], mods[0], n_lat)
    for i in range(depth):
        last = i == depth - 1
        lam_init = 0.8 - 0.6 * math.exp(-0.3 * i)
        mod = mods[i]
        qk = _proj(xn, w_in_bf, i, [(o_q, 2 * QK_A)], BF16, 256, rope_q_cols=QK_A, rope=(cos_t, sin_t))
        vt = _proj(xn, w_in_bf, i, [(o_v, V_A)], BF16, 512, transposed=True)
        zz = _proj(xn, w_in_bf, i, [(o_z, V_A), (o_zb, V_B)], BF16, 512)
        pb = _proj(xn, w_in_bf, i, [(o_b, CONV_CH)], F32, 512)
        ab = _proj(xn, w_ab, i, [(0, LANES)], F32, LANES)

        lam_p = diff_lambda[i]
        sw = subln_w[i][None, :]
        ya = _attention(qk, zz, vt, lam_p, sw, lam_init, q_row0=0, n_q=n_lat, k_row0=0, n_keys=t,
                        tq=512, tk=768)
        if not last:
            ya_ctx = _attention(qk, zz, vt, lam_p, sw, lam_init, q_row0=n_lat, n_q=n_ctx, k_row0=n_lat,
                                n_keys=n_ctx, tq=256, tk=256)

        qkv = _conv_prep(pb, conv_w[i], n_lat, n_ctx)
        ab_t = ab[:, 0:4 * B_HEADS].T
        alog_b = jnp.broadcast_to(dn_a_log[i].reshape(2 * B_HEADS, 1), (2 * B_HEADS, DELTA_CHUNK))
        dtb_b = jnp.broadcast_to(dn_dt_bias[i].reshape(2 * B_HEADS, 1), (2 * B_HEADS, DELTA_CHUNK))
        gates = _gates(ab_t, alog_b, dtb_b)
        g4 = gates.reshape(4, B_HEADS, nch, DELTA_CHUNK).transpose(1, 2, 0, 3)
        g4 = jnp.concatenate([g4, jnp.zeros_like(g4)], axis=2)
        aq, bo, ee = _delta_prep(qkv, g4)
        out_f, out_b = _delta_scan(aq, bo, ee, n_lat, n_ctx)
        yb = _delta_post(out_f, out_b, zz, dn_norm_w[i][None, :])

        if not last:
            xcat, xn = _out_proj(ya, ya_ctx, yb, w_out_bf, i, xcat, mod, norm_w[i + 1][None, :], mods[i + 1], n_lat)
        else:
            out = _out_proj_final(ya, yb, w_out_bf, i, xcat, mod, final_norm_w[None, :], n_lat)
    return out[None]
```

```python
import functools
import math

import jax
import jax.numpy as jnp
from jax import lax
from jax.experimental import pallas as pl
from jax.experimental.pallas import tpu as pltpu

F32 = jnp.float32
BF16 = jnp.bfloat16

LANES = 128
VMEM_LIMIT_BYTES = 56 * 1024 * 1024

D_MODEL = 2048
GRID_W = 64
A_HEADS = 8
A_HEAD_DIM = 64
A_VAL_DIM = 2 * A_HEAD_DIM
QK_A = A_HEADS * 2 * A_HEAD_DIM
V_A = A_HEADS * A_VAL_DIM
B_HEADS = 8
B_DK = 128
B_DV = 128
QK_B = B_HEADS * B_DK
V_B = B_HEADS * B_DV
CONV_K = 5
CONV_CH = 2 * QK_B + V_B
ROPE_THETA = 10000.0
NORM_EPS = 1e-6
SUBLN_EPS = 1e-5
L2_EPS = 1e-6

DELTA_CHUNK = 128
V_ROWS = A_VAL_DIM + 16
NEG_BIG = -1e30
Q_SCALE = A_HEAD_DIM ** -0.5 * math.log2(math.e)


def _params(*sem, flags=None):
    return pltpu.CompilerParams(dimension_semantics=sem, vmem_limit_bytes=VMEM_LIMIT_BYTES, flags=flags)


def _silu(x):
    return x * jax.nn.sigmoid(x)


def _tile(n, preferred, fallback):
    return preferred if n % preferred == 0 else fallback


def _ada_kernel(c_ref, w_ref, b_ref, o_ref):
    s = _silu(c_ref[...])
    o_ref[0] = jnp.dot(s.astype(BF16), w_ref[0].astype(BF16), preferred_element_type=F32) + b_ref[0]


def _ada_all(cc, w_ada, b_ada):
    depth, d, n = w_ada.shape
    tn = 512
    return pl.pallas_call(
        _ada_kernel,
        grid=(depth, n // tn),
        in_specs=[pl.BlockSpec((8, d), lambda l, j: (0, 0)),
                  pl.BlockSpec((1, d, tn), lambda l, j: (l, 0, j)),
                  pl.BlockSpec((1, 1, tn), lambda l, j: (l, 0, j))],
        out_specs=pl.BlockSpec((1, 8, tn), lambda l, j: (l, 0, j)),
        out_shape=jax.ShapeDtypeStruct((depth, 8, n), F32),
        compiler_params=_params("parallel", "parallel"),
    )(cc, w_ada, b_ada.reshape(depth, 1, n))


def _modulated_norm(x, nw, shift, scale):
    ms = jnp.mean(x * x, axis=-1, keepdims=True)
    return (x * lax.rsqrt(ms + NORM_EPS) * nw) * (1.0 + scale) + shift


def _norm_kernel(x_ref, nw_ref, mod_ref, o_ref, *, n_lat_tiles):
    is_ctx = pl.program_id(0) >= n_lat_tiles
    d = x_ref.shape[1]
    shift = jnp.where(is_ctx, mod_ref[1:2, 0:d], mod_ref[0:1, 0:d])
    scale = jnp.where(is_ctx, mod_ref[1:2, d:2 * d], mod_ref[0:1, d:2 * d])
    o_ref[...] = _modulated_norm(x_ref[...], nw_ref[...], shift, scale).astype(o_ref.dtype)


def _norm_mod(xcat, nw, mod, n_lat):
    t, d = xcat.shape
    tm = 256
    return pl.pallas_call(
        functools.partial(_norm_kernel, n_lat_tiles=n_lat // tm),
        grid=(t // tm,),
        in_specs=[pl.BlockSpec((tm, d), lambda i: (i, 0)),
                  pl.BlockSpec((1, d), lambda i: (0, 0)),
                  pl.BlockSpec((8, 2 * d), lambda i: (0, 0))],
        out_specs=pl.BlockSpec((tm, d), lambda i: (i, 0)),
        out_shape=jax.ShapeDtypeStruct((t, d), BF16),
        compiler_params=_params("parallel"),
    )(xcat, nw, mod)


def _mm_kernel(x_ref, w_ref, o_ref):
    o_ref[...] = jnp.dot(x_ref[...], w_ref[0], preferred_element_type=F32).astype(o_ref.dtype)


def _mm_t_kernel(x_ref, w_ref, o_ref, acc_ref):
    acc_ref[...] = jnp.dot(x_ref[...], w_ref[0], preferred_element_type=F32)
    o_ref[...] = acc_ref[...].T.astype(o_ref.dtype)


def _mm_rope_kernel(x_ref, w_ref, cos_ref, sin_ref, o_ref, *, q_tiles):
    scale = jnp.where(pl.program_id(1) < q_tiles, Q_SCALE, 1.0)
    t = jnp.dot(x_ref[...], w_ref[0], preferred_element_type=F32) * scale
    c = cos_ref[...]
    s = sin_ref[...]
    lane = lax.broadcasted_iota(jnp.int32, c.shape, 1)
    first = (lane % A_HEAD_DIM) < (A_HEAD_DIM // 2)
    for a in range(0, t.shape[1], LANES):
        th = t[:, a:a + LANES]
        sw = jnp.where(first, pltpu.roll(th, LANES - A_HEAD_DIM // 2, 1), pltpu.roll(th, A_HEAD_DIM // 2, 1))
        o_ref[:, a:a + LANES] = (th * c + sw * s).astype(o_ref.dtype)


def _proj(xn, w, layer, col_segments, out_dtype, tn, rope_q_cols=None, rope=None, transposed=False):
    t, d = xn.shape
    tm = _tile(t, 1408, 256)
    assert all(c0 % tn == 0 and width % tn == 0 for c0, width in col_segments)
    n_tiles = sum(width // tn for _, width in col_segments)
    scratch = []
    bounds, offs, acc = [], [], 0
    for c0, width in col_segments:
        offs.append(c0 // tn - acc)
        acc += width // tn
        bounds.append(acc)

    def wcol(j):
        col = j + offs[-1]
        for bnd, off in zip(reversed(bounds[:-1]), reversed(offs[:-1])):
            col = jnp.where(j < bnd, j + off, col)
        return col

    in_specs = [pl.BlockSpec((tm, d), lambda i, j: (i, 0)),
                pl.BlockSpec((1, d, tn), lambda i, j: (layer, 0, wcol(j)))]
    args = [xn, w]
    kern = _mm_kernel
    out_spec = pl.BlockSpec((tm, tn), lambda i, j: (i, j))
    out_shape = (t, n_tiles * tn)
    if rope is not None:
        in_specs += [pl.BlockSpec((tm, LANES), lambda i, j: (i, 0))] * 2
        args += list(rope)
        kern = functools.partial(_mm_rope_kernel, q_tiles=rope_q_cols // tn)
    if transposed:
        kern = _mm_t_kernel
        out_spec = pl.BlockSpec((tn, tm), lambda i, j: (j, i))
        out_shape = (n_tiles * tn, t)
        scratch = [pltpu.VMEM((tm, tn), F32)]
    return pl.pallas_call(
        kern,
        grid=(t // tm, n_tiles),
        in_specs=in_specs,
        out_specs=out_spec,
        out_shape=jax.ShapeDtypeStruct(out_shape, out_dtype),
        scratch_shapes=scratch,
        compiler_params=_params("parallel", "parallel"),
    )(*args)


def _attn_kernel(q_ref, k_ref, vt_ref, z_ref, lam_ref, sw_ref, o_ref, *, tk, lam_init):
    tq = q_ref.shape[0]
    nkeys = k_ref.shape[0]
    q = q_ref[...]
    lane = lax.broadcasted_iota(jnp.int32, q.shape, 1)
    zero = jnp.zeros_like(q)
    qz = jnp.concatenate([jnp.where(lane < A_HEAD_DIM, q, zero),
                          jnp.where(lane >= A_HEAD_DIM, q, zero)], axis=0)
    m = jnp.full((1, 2 * tq), NEG_BIG, F32)
    acc = jnp.zeros((V_ROWS, 2 * tq), F32)
    tail_row = lax.broadcasted_iota(jnp.int32, (V_ROWS - A_VAL_DIM, tk), 0)
    tail = jnp.where(tail_row == 0, 1.0, 0.0).astype(BF16)

    def scores(c0):
        return lax.dot_general(k_ref[c0:c0 + tk, :], qz, (((1,), (1,)), ((), ())),
                               preferred_element_type=F32)

    st_next = scores(0)
    for c0 in range(0, nkeys, tk):
        st = st_next
        if c0 + tk < nkeys:
            st_next = scores(c0 + tk)
        m_new = jnp.maximum(m, jnp.max(st, axis=0, keepdims=True))
        alpha = jnp.exp2(m - m_new)
        p = jnp.exp2(st - m_new).astype(BF16)
        v_ext = jnp.concatenate([vt_ref[:, c0:c0 + tk], tail], axis=0)
        acc = alpha * acc + jnp.dot(v_ext, p, preferred_element_type=F32)
        m = m_new
    o1 = acc[0:A_VAL_DIM, 0:tq] / acc[A_VAL_DIM:A_VAL_DIM + 1, 0:tq]
    o2 = acc[0:A_VAL_DIM, tq:] / acc[A_VAL_DIM:A_VAL_DIM + 1, tq:]
    lp = lam_ref[...]
    lam = (jnp.exp(jnp.sum(lp[0:1] * lp[1:2], axis=1, keepdims=True))
           - jnp.exp(jnp.sum(lp[2:3] * lp[3:4], axis=1, keepdims=True)) + lam_init)
    o = (o1 - lam * o2).T
    ms = jnp.mean(o * o, axis=-1, keepdims=True)
    y = (o * lax.rsqrt(ms + SUBLN_EPS) * sw_ref[...]) * (1.0 - lam_init)
    o_ref[...] = (y * _silu(z_ref[...].astype(F32))).astype(o_ref.dtype)


def _attention(qk, zz, vt, lam_p, subln_w, lam_init, *, q_row0, n_q, k_row0, n_keys, tq, tk):
    qb0 = q_row0 // tq
    kb0 = k_row0 // n_keys
    return pl.pallas_call(
        functools.partial(_attn_kernel, tk=tk, lam_init=lam_init),
        grid=(A_HEADS, n_q // tq),
        in_specs=[pl.BlockSpec((tq, LANES), lambda h, i: (qb0 + i, h)),
                  pl.BlockSpec((n_keys, LANES), lambda h, i: (kb0, A_HEADS + h)),
                  pl.BlockSpec((A_VAL_DIM, n_keys), lambda h, i: (h, kb0)),
                  pl.BlockSpec((tq, LANES), lambda h, i: (qb0 + i, h)),
                  pl.BlockSpec(lam_p.shape, lambda h, i: (0, 0)),
                  pl.BlockSpec((1, A_VAL_DIM), lambda h, i: (0, 0))],
        out_specs=pl.BlockSpec((tq, LANES), lambda h, i: (i, h)),
        out_shape=jax.ShapeDtypeStruct((n_q, V_A), BF16),
        compiler_params=_params("parallel", "parallel"),
    )(qk, qk, vt, zz, lam_p, subln_w)


def _conv_kernel(x_ref, w_ref, o_ref, pad_ref, *, n_lat, n_ctx, rows):
    j = pl.program_id(0)
    half = (CONV_K - 1) // 2
    lat0 = 8
    ctx0 = lat0 + n_lat + 16
    zeros8 = jnp.zeros((8, LANES), F32)
    pad_ref[0:8, :] = zeros8
    pad_ref[lat0 + n_lat:lat0 + n_lat + 8, :] = zeros8
    pad_ref[lat0 + n_lat + 8:ctx0, :] = zeros8
    pad_ref[ctx0 + n_ctx:ctx0 + n_ctx + 8, :] = zeros8
    pad_ref[lat0:lat0 + n_lat, :] = x_ref[0:n_lat, :].astype(F32)
    pad_ref[ctx0:ctx0 + n_ctx, :] = x_ref[n_lat:n_lat + n_ctx, :].astype(F32)
    w = w_ref[...]
    is_qk = j < 2 * B_HEADS
    post = jnp.where(j < B_HEADS, B_DK ** -0.5, 1.0)
    for src0, dst0, n in ((lat0, 0, n_lat), (ctx0, n_lat, n_ctx)):
        for r in range(0, n, rows):
            acc = jnp.zeros((rows, LANES), F32)
            for tap in range(CONV_K):
                a = src0 + r + tap - half
                acc = acc + pad_ref[a:a + rows, :] * w[tap:tap + 1, :]
            y = _silu(acc)
            nrm = lax.rsqrt(jnp.sum(y * y, axis=-1, keepdims=True) + L2_EPS) * post
            y = y * jnp.where(is_qk, nrm, 1.0)
            o_ref[dst0 + r:dst0 + r + rows, :] = y.astype(o_ref.dtype)


def _conv_prep(pb, conv_w, n_lat, n_ctx):
    t = pb.shape[0]
    nblk = CONV_CH // LANES
    return pl.pallas_call(
        functools.partial(_conv_kernel, n_lat=n_lat, n_ctx=n_ctx, rows=256),
        grid=(nblk,),
        in_specs=[pl.BlockSpec((t, LANES), lambda j: (0, j)),
                  pl.BlockSpec((CONV_K, LANES), lambda j: (0, j))],
        out_specs=pl.BlockSpec((t, LANES), lambda j: (0, j)),
        out_shape=jax.ShapeDtypeStruct((t, CONV_CH), BF16),
        scratch_shapes=[pltpu.VMEM((t + 40, LANES), F32)],
        compiler_params=_params("parallel"),
    )(pb, conv_w)


def _gates_kernel(ab_ref, alog_ref, dtb_ref, o_ref):
    c = DELTA_CHUNK
    nh = B_HEADS
    r = lax.broadcasted_iota(jnp.int32, (c, c), 0)
    l = lax.broadcasted_iota(jnp.int32, (c, c), 1)
    upper = (r <= l).astype(F32)
    lower = (r >= l).astype(F32)
    neg_a = -jnp.exp(alog_ref[...])
    dtb = dtb_ref[...]
    for t0 in range(0, ab_ref.shape[1], c):
        x = ab_ref[0:2 * nh, t0:t0 + c] + dtb
        sp = jnp.maximum(x, 0.0) + jnp.log1p(jnp.exp(-jnp.abs(x)))
        g = neg_a * sp
        o_ref[0:nh, t0:t0 + c] = jnp.dot(g[0:nh], upper, preferred_element_type=F32,
                                          precision=lax.Precision.HIGHEST)
        o_ref[nh:2 * nh, t0:t0 + c] = jnp.dot(g[nh:2 * nh], lower, preferred_element_type=F32,
                                               precision=lax.Precision.HIGHEST)
        o_ref[2 * nh:4 * nh, t0:t0 + c] = jax.nn.sigmoid(ab_ref[2 * nh:4 * nh, t0:t0 + c])


def _gates(ab_t, alog_b, dtb_b):
    rows, t = ab_t.shape
    tl = _tile(t, 1408, 256)
    return pl.pallas_call(
        _gates_kernel,
        grid=(t // tl,),
        in_specs=[pl.BlockSpec((rows, tl), lambda i: (0, i)),
                  pl.BlockSpec(alog_b.shape, lambda i: (0, 0)),
                  pl.BlockSpec(dtb_b.shape, lambda i: (0, 0))],
        out_specs=pl.BlockSpec((rows, tl), lambda i: (0, i)),
        out_shape=jax.ShapeDtypeStruct((rows, t), F32),
        compiler_params=_params("parallel"),
    )(ab_t, alog_b, dtb_b)


def _delta_prep_kernel(q_ref, k_ref, v_ref, g_ref, aq_ref, bo_ref, e_ref, *, n_chunks):
    c = DELTA_CHUNK
    row = lax.broadcasted_iota(jnp.int32, (c, c), 0)
    col = lax.broadcasted_iota(jnp.int32, (c, c), 1)
    eye = row == col
    eye_f = eye.astype(F32)
    n_levels = int(math.log2(c))
    same_blk = [(row >> s) == (col >> s) for s in range(n_levels + 1)]
    off_blk = [None] + [jnp.logical_and(same_blk[lv + 1], jnp.logical_not(same_blk[lv]))
                        for lv in range(1, n_levels)]
    nt = (((1,), (1,)), ((), ()))

    def to_col(r):
        return jnp.sum(jnp.where(eye, jnp.broadcast_to(r, (c, c)), 0.0), axis=1, keepdims=True)

    chains = []
    for ci in range(n_chunks):
        q = q_ref[ci * c:(ci + 1) * c, :]
        k = k_ref[ci * c:(ci + 1) * c, :]
        v = v_ref[ci * c:(ci + 1) * c, :]
        rows = g_ref[0, ci]
        kk = lax.dot_general(k, k, nt, preferred_element_type=F32)
        qk = lax.dot_general(q, k, nt, preferred_element_type=F32)
        for d in range(2):
            if d == 0:
                g_r, b_r = rows[0:1], rows[2:3]
                g_last = g_r[:, c - 1:c]
                mask, strict = row >= col, row > col
            else:
                g_r, b_r = rows[1:2], rows[3:4]
                g_last = g_r[:, 0:1]
                mask, strict = row <= col, row < col
            g_c = to_col(g_r)
            b_c = to_col(b_r)
            decay = jnp.where(mask, jnp.exp(jnp.where(mask, g_c - g_r, 0.0)), 0.0)
            n_mat = jnp.where(strict, kk * b_c * decay, 0.0)
            chains.append(dict(ci=ci, d=d, q=q, k=k, v=v, qk=qk, g_c=g_c, b_c=b_c, g_last=g_last, mask=mask,
                               decay=decay, n_mat=n_mat, x=eye_f - jnp.where(same_blk[1], n_mat, 0.0)))
    for lv in range(1, n_levels):
        for ch in chains:
            ch["xb"] = ch["x"].astype(BF16)
            ch["a"] = jnp.dot(ch["xb"], jnp.where(off_blk[lv], ch["n_mat"], 0.0).astype(BF16),
                              preferred_element_type=F32)
        for ch in chains:
            ch["x"] = ch["x"] - jnp.dot(ch["a"].astype(BF16), ch["xb"], preferred_element_type=F32)
    for ch in chains:
        kf = ch["k"].astype(F32)
        ch["e_g"] = jnp.exp(ch["g_c"])
        kb = kf * ch["b_c"]
        rhs = jnp.concatenate([(kb * ch["e_g"]).astype(BF16),
                               (ch["v"].astype(F32) * ch["b_c"]).astype(BF16)], axis=1)
        ch["wu"] = jnp.dot(ch["x"].astype(BF16), rhs, preferred_element_type=F32).astype(BF16)
        ch["attn"] = jnp.where(ch["mask"], ch["qk"] * ch["decay"], 0.0).astype(BF16)
        ch["ke_t"] = (kf * jnp.exp(ch["g_last"] - ch["g_c"])).T.astype(BF16)
    for ch in chains:
        ch["m1"] = jnp.dot(ch["ke_t"], ch["wu"], preferred_element_type=F32)
        ch["m2"] = jnp.dot(ch["attn"], ch["wu"], preferred_element_type=F32)
    for ch in chains:
        ci, d, m1, m2 = ch["ci"], ch["d"], ch["m1"], ch["m2"]
        aq_ref[d, 0, ci, 0:c, :] = (-m1[:, 0:B_DV]).astype(BF16)
        aq_ref[d, 0, ci, c:2 * c, :] = (ch["q"].astype(F32) * ch["e_g"] - m2[:, 0:B_DV]).astype(BF16)
        bo_ref[d, 0, ci, 0:c, :] = m1[:, B_DV:].astype(BF16)
        bo_ref[d, 0, ci, c:2 * c, :] = m2[:, B_DV:].astype(BF16)
    for ci in range(n_chunks):
        e_f, e_b = (jnp.broadcast_to(jnp.exp(chains[2 * ci + d]["g_last"]), (1, c)) for d in range(2))
        e_ref[0, ci] = jnp.concatenate([e_f, e_b, jnp.zeros((6, c), F32)], axis=0)


def _delta_prep(qkv, g4):
    t = qkv.shape[0]
    c = DELTA_CHUNK
    nch = t // c
    g = 6
    rows = g * c
    return pl.pallas_call(
        functools.partial(_delta_prep_kernel, n_chunks=g),
        grid=(B_HEADS, nch // g),
        in_specs=[pl.BlockSpec((rows, LANES), lambda h, i: (i, h)),
                  pl.BlockSpec((rows, LANES), lambda h, i: (i, B_HEADS + h)),
                  pl.BlockSpec((rows, LANES), lambda h, i: (i, 2 * B_HEADS + h)),
                  pl.BlockSpec((1, g, 8, c), lambda h, i: (h, i, 0, 0))],
        out_specs=[pl.BlockSpec((2, 1, g, 2 * c, B_DV), lambda h, i: (0, h, i, 0, 0)),
                   pl.BlockSpec((2, 1, g, 2 * c, B_DV), lambda h, i: (0, h, i, 0, 0)),
                   pl.BlockSpec((1, g, 8, c), lambda h, i: (h, i, 0, 0))],
        out_shape=[jax.ShapeDtypeStruct((2, B_HEADS, nch, 2 * c, B_DV), BF16),
                   jax.ShapeDtypeStruct((2, B_HEADS, nch, 2 * c, B_DV), BF16),
                   jax.ShapeDtypeStruct((B_HEADS, nch, 8, c), F32)],
        compiler_params=_params("parallel", "parallel"),
    )(qkv, qkv, qkv, g4)


def _delta_scan_kernel(aqf_ref, bof_ref, ef_ref, aqb_ref, bob_ref, eb_ref, of_ref, ob_ref, st_ref):
    c = DELTA_CHUNK

    @pl.when(pl.program_id(0) == 0)
    def _():
        st_ref[...] = jnp.zeros_like(st_ref)

    for d, (aq_ref, bo_ref, e_ref, o_ref) in enumerate(((aqf_ref, bof_ref, ef_ref, of_ref),
                                                        (aqb_ref, bob_ref, eb_ref, ob_ref))):
        for h in range(B_HEADS):
            state = st_ref[d, h]
            res = jnp.dot(aq_ref[0, h, 0], state.astype(BF16), preferred_element_type=F32)
            bo = bo_ref[0, h, 0].astype(F32)
            o_ref[:, h * B_DV:(h + 1) * B_DV] = res[c:] + bo[c:]
            st_ref[d, h] = state * e_ref[h, 0, d:d + 1, 0:1] + res[0:c] + bo[0:c]


def _delta_scan(aq, bo, ee, n_lat, n_ctx):
    c = DELTA_CHUNK
    nch = aq.shape[2]
    n_lat_c = n_lat // c
    n_ctx_c = n_ctx // c
    t = nch * c

    def fwd(s):
        return jnp.where(s < n_ctx_c, s + n_lat_c, s - n_ctx_c)

    def bwd(s):
        return nch - 1 - s

    def big(direction, order):
        return pl.BlockSpec((1, B_HEADS, 1, 2 * c, B_DV), lambda s: (direction, 0, order(s), 0, 0))

    def small(order):
        return pl.BlockSpec((B_HEADS, 1, 8, c), lambda s: (0, order(s), 0, 0))

    return pl.pallas_call(
        _delta_scan_kernel,
        grid=(nch,),
        in_specs=[big(0, fwd), big(0, fwd), small(fwd), big(1, bwd), big(1, bwd), small(bwd)],
        out_specs=[pl.BlockSpec((c, V_B), lambda s: (fwd(s), 0)),
                   pl.BlockSpec((c, V_B), lambda s: (bwd(s), 0))],
        out_shape=[jax.ShapeDtypeStruct((t, V_B), F32), jax.ShapeDtypeStruct((t, V_B), F32)],
        scratch_shapes=[pltpu.VMEM((2, B_HEADS, B_DK, B_DV), F32)],
        compiler_params=_params("arbitrary"),
    )(aq, bo, ee, aq, bo, ee)


def _delta_post_kernel(of_ref, ob_ref, z_ref, nw_ref, y_ref):
    nw = nw_ref[...]
    for h in range(B_HEADS):
        sl = slice(h * B_DV, (h + 1) * B_DV)
        o = of_ref[:, sl] + ob_ref[:, sl]
        ms = jnp.mean(o * o, axis=-1, keepdims=True)
        y = (o * lax.rsqrt(ms + NORM_EPS) * nw) * _silu(z_ref[:, sl].astype(F32))
        y_ref[:, sl] = y.astype(y_ref.dtype)


def _delta_post(o_f, o_b, zz, dn_norm_w):
    t = o_f.shape[0]
    tm = 256
    zb_blk = V_A // V_B
    return pl.pallas_call(
        _delta_post_kernel,
        grid=(t // tm,),
        in_specs=[pl.BlockSpec((tm, V_B), lambda i: (i, 0)),
                  pl.BlockSpec((tm, V_B), lambda i: (i, 0)),
                  pl.BlockSpec((tm, V_B), lambda i: (i, zb_blk)),
                  pl.BlockSpec((1, B_DV), lambda i: (0, 0))],
        out_specs=pl.BlockSpec((tm, V_B), lambda i: (i, 0)),
        out_shape=jax.ShapeDtypeStruct((t, V_B), BF16),
        compiler_params=_params("parallel"),
    )(o_f, o_b, zz, dn_norm_w)


def _out_kernel(ya_ref, yac_ref, yb_ref, wa_ref, wb_ref, x_ref, mod_ref, nw_ref, modn_ref, o_ref, xn_ref, *,
                n_lat_tiles):
    is_ctx = pl.program_id(0) >= n_lat_tiles
    d = x_ref.shape[1]
    gate = jnp.where(is_ctx, mod_ref[1:2, :], mod_ref[0:1, :])
    ya = jnp.where(is_ctx, yac_ref[...], ya_ref[...])
    acc = jnp.dot(ya, wa_ref[0], preferred_element_type=F32)
    acc = acc + jnp.dot(yb_ref[...], wb_ref[0], preferred_element_type=F32)
    x = x_ref[...] + gate * acc
    o_ref[...] = x
    shift = jnp.where(is_ctx, modn_ref[1:2, 0:d], modn_ref[0:1, 0:d])
    scale = jnp.where(is_ctx, modn_ref[1:2, d:2 * d], modn_ref[0:1, d:2 * d])
    xn_ref[...] = _modulated_norm(x, nw_ref[...], shift, scale).astype(xn_ref.dtype)


def _out_final_kernel(ya_ref, yb_ref, wa_ref, wb_ref, x_ref, mod_ref, fw_ref, o_ref):
    acc = jnp.dot(ya_ref[...], wa_ref[0], preferred_element_type=F32)
    acc = acc + jnp.dot(yb_ref[...], wb_ref[0], preferred_element_type=F32)
    x = x_ref[...] + mod_ref[0:1, :] * acc
    ms = jnp.mean(x * x, axis=-1, keepdims=True)
    o_ref[...] = x * lax.rsqrt(ms + NORM_EPS) * fw_ref[...]


def _out_proj(ya, ya_ctx, yb, w_out, layer, xcat, mod, nw_next, mod_next, n_lat):
    t, d = xcat.shape
    tm = 256
    n_lat_tiles = n_lat // tm
    return pl.pallas_call(
        functools.partial(_out_kernel, n_lat_tiles=n_lat_tiles),
        grid=(t // tm,),
        in_specs=[pl.BlockSpec((tm, V_A), lambda i: (jnp.minimum(i, n_lat_tiles - 1), 0)),
                  pl.BlockSpec((tm, V_A), lambda i: (jnp.maximum(i - n_lat_tiles, 0), 0)),
                  pl.BlockSpec((tm, V_B), lambda i: (i, 0)),
                  pl.BlockSpec((1, V_A, d), lambda i: (layer, 0, 0)),
                  pl.BlockSpec((1, V_B, d), lambda i: (layer, V_A // V_B, 0)),
                  pl.BlockSpec((tm, d), lambda i: (i, 0)),
                  pl.BlockSpec((8, d), lambda i: (0, 2)),
                  pl.BlockSpec((1, d), lambda i: (0, 0)),
                  pl.BlockSpec((8, 2 * d), lambda i: (0, 0))],
        out_specs=[pl.BlockSpec((tm, d), lambda i: (i, 0)),
                   pl.BlockSpec((tm, d), lambda i: (i, 0))],
        out_shape=[jax.ShapeDtypeStruct((t, d), F32), jax.ShapeDtypeStruct((t, d), BF16)],
        compiler_params=_params("parallel"),
    )(ya, ya_ctx, yb, w_out, w_out, xcat, mod, nw_next, mod_next)


def _out_proj_final(ya, yb, w_out, layer, xcat, mod, final_w, n_lat):
    d = xcat.shape[1]
    tm = 256
    return pl.pallas_call(
        _out_final_kernel,
        grid=(n_lat // tm,),
        in_specs=[pl.BlockSpec((tm, V_A), lambda i: (i, 0)),
                  pl.BlockSpec((tm, V_B), lambda i: (i, 0)),
                  pl.BlockSpec((1, V_A, d), lambda i: (layer, 0, 0)),
                  pl.BlockSpec((1, V_B, d), lambda i: (layer, V_A // V_B, 0)),
                  pl.BlockSpec((tm, d), lambda i: (i, 0)),
                  pl.BlockSpec((8, d), lambda i: (0, 2)),
                  pl.BlockSpec((1, d), lambda i: (0, 0))],
        out_specs=pl.BlockSpec((tm, d), lambda i: (i, 0)),
        out_shape=jax.ShapeDtypeStruct((n_lat, d), F32),
        compiler_params=_params("parallel"),
    )(ya, yb, w_out, w_out, xcat, mod, final_w)


def _rope_tables(n_lat, n_ctx):
    n_rows = n_lat // GRID_W
    rows = jnp.repeat(jnp.arange(n_rows, dtype=F32), GRID_W)
    cols = jnp.tile(jnp.arange(GRID_W, dtype=F32), n_rows)
    n_freq = A_HEAD_DIM // 4
    inv = ROPE_THETA ** (-jnp.arange(n_freq, dtype=F32) / n_freq)
    ang = jnp.concatenate([rows[:, None] * inv, cols[:, None] * inv], axis=-1)
    cos = jnp.cos(ang)
    sin = jnp.sin(ang)
    cos_t = jnp.tile(cos, (1, LANES // (A_HEAD_DIM // 2)))
    sin_t = jnp.tile(jnp.concatenate([-sin, sin], axis=-1), (1, LANES // A_HEAD_DIM))
    cos_t = jnp.concatenate([cos_t, jnp.ones((n_ctx, LANES), F32)], axis=0)
    sin_t = jnp.concatenate([sin_t, jnp.zeros((n_ctx, LANES), F32)], axis=0)
    return cos_t, sin_t


def kernel(x, c, ctx, c_ctx, w_ada, b_ada, norm_w, w_in, conv_w, dn_a_log, dn_dt_bias, dn_norm_w,
           diff_lambda, subln_w, w_out, final_norm_w):
    depth = w_ada.shape[0]
    n_lat = x.shape[1]
    n_ctx = ctx.shape[1]
    t = n_lat + n_ctx
    d = x.shape[2]
    assert x.shape[0] == 1 and d == D_MODEL
    assert n_lat % 512 == 0 and n_ctx == 256 and t % 768 == 0

    xcat = jnp.concatenate([x[0], ctx[0]], axis=0)
    cc = jnp.zeros((8, d), F32).at[0].set(c[0]).at[1].set(c_ctx)
    mods = _ada_all(cc, w_ada, b_ada)
    cos_t, sin_t = _rope_tables(n_lat, n_ctx)

    o_q, o_v, o_z, o_b, o_zb, o_ab = 0, 2 * QK_A, 2 * QK_A + V_A, 2 * QK_A + 2 * V_A, \
        2 * QK_A + 2 * V_A + CONV_CH, 2 * QK_A + 2 * V_A + CONV_CH + V_B
    w_in_bf = w_in.astype(BF16)
    w_out_bf = w_out.astype(BF16)
    w_ab = jnp.pad(w_in[:, :, o_ab:], ((0, 0), (0, 0), (0, LANES - 4 * B_HEADS))).astype(BF16)

    nch = t // DELTA_CHUNK
    xn = _norm_mod(xcat, norm_w[0][None, :], mods[0], n_lat)
    for i in range(depth):
        last = i == depth - 1
        lam_init = 0.8 - 0.6 * math.exp(-0.3 * i)
        mod = mods[i]
        qk = _proj(xn, w_in_bf, i, [(o_q, 2 * QK_A)], BF16, 256, rope_q_cols=QK_A, rope=(cos_t, sin_t))
        vt = _proj(xn, w_in_bf, i, [(o_v, V_A)], BF16, 512, transposed=True)
        zz = _proj(xn, w_in_bf, i, [(o_z, V_A), (o_zb, V_B)], BF16, 512)
        pb = _proj(xn, w_in_bf, i, [(o_b, CONV_CH)], BF16, 512)
        ab = _proj(xn, w_ab, i, [(0, LANES)], F32, LANES)

        lam_p = diff_lambda[i]
        sw = subln_w[i][None, :]
        ya = _attention(qk, zz, vt, lam_p, sw, lam_init, q_row0=0, n_q=n_lat, k_row0=0, n_keys=t,
                        tq=_tile(n_lat, 1024, 512), tk=768)
        if not last:
            ya_ctx = _attention(qk, zz, vt, lam_p, sw, lam_init, q_row0=n_lat, n_q=n_ctx, k_row0=n_lat,
                                n_keys=n_ctx, tq=256, tk=256)

        qkv = _conv_prep(pb, conv_w[i], n_lat, n_ctx)
        ab_t = ab[:, 0:4 * B_HEADS].T
        alog_b = jnp.broadcast_to(dn_a_log[i].reshape(2 * B_HEADS, 1), (2 * B_HEADS, DELTA_CHUNK))
        dtb_b = jnp.broadcast_to(dn_dt_bias[i].reshape(2 * B_HEADS, 1), (2 * B_HEADS, DELTA_CHUNK))
        gates = _gates(ab_t, alog_b, dtb_b)
        g4 = gates.reshape(4, B_HEADS, nch, DELTA_CHUNK).transpose(1, 2, 0, 3)
        g4 = jnp.concatenate([g4, jnp.zeros_like(g4)], axis=2)
        aq, bo, ee = _delta_prep(qkv, g4)
        out_f, out_b = _delta_scan(aq, bo, ee, n_lat, n_ctx)
        yb = _delta_post(out_f, out_b, zz, dn_norm_w[i][None, :])

        if not last:
            xcat, xn = _out_proj(ya, ya_ctx, yb, w_out_bf, i, xcat, mod, norm_w[i + 1][None, :], mods[i + 1], n_lat)
        else:
            out = _out_proj_final(ya, yb, w_out_bf, i, xcat, mod, final_norm_w[None, :], n_lat)
    return out[None]
```

```python
import functools
import math

import jax
import jax.numpy as jnp
from jax import lax
from jax.experimental import pallas as pl
from jax.experimental.pallas import tpu as pltpu

F32 = jnp.float32
BF16 = jnp.bfloat16

LANES = 128
VMEM_LIMIT_BYTES = 56 * 1024 * 1024

D_MODEL = 2048
GRID_W = 64
A_HEADS = 8
A_HEAD_DIM = 64
A_VAL_DIM = 2 * A_HEAD_DIM
QK_A = A_HEADS * 2 * A_HEAD_DIM
V_A = A_HEADS * A_VAL_DIM
B_HEADS = 8
B_DK = 128
B_DV = 128
QK_B = B_HEADS * B_DK
V_B = B_HEADS * B_DV
CONV_K = 5
CONV_CH = 2 * QK_B + V_B
ROPE_THETA = 10000.0
NORM_EPS = 1e-6
SUBLN_EPS = 1e-5
L2_EPS = 1e-6

DELTA_CHUNK = 128
V_ROWS = A_VAL_DIM + 16
NEG_BIG = -1e30
Q_SCALE = A_HEAD_DIM ** -0.5 * math.log2(math.e)


def _params(*sem, flags=None):
    return pltpu.CompilerParams(dimension_semantics=sem, vmem_limit_bytes=VMEM_LIMIT_BYTES, flags=flags)


def _silu(x):
    return x * jax.nn.sigmoid(x)


def _tile(n, preferred, fallback):
    return preferred if n % preferred == 0 else fallback


def _ada_kernel(c_ref, w_ref, b_ref, o_ref):
    s = _silu(c_ref[...])
    o_ref[0] = jnp.dot(s.astype(BF16), w_ref[0].astype(BF16), preferred_element_type=F32) + b_ref[0]


def _ada_all(cc, w_ada, b_ada):
    depth, d, n = w_ada.shape
    tn = 512
    return pl.pallas_call(
        _ada_kernel,
        grid=(depth, n // tn),
        in_specs=[pl.BlockSpec((8, d), lambda l, j: (0, 0)),
                  pl.BlockSpec((1, d, tn), lambda l, j: (l, 0, j)),
                  pl.BlockSpec((1, 1, tn), lambda l, j: (l, 0, j))],
        out_specs=pl.BlockSpec((1, 8, tn), lambda l, j: (l, 0, j)),
        out_shape=jax.ShapeDtypeStruct((depth, 8, n), F32),
        compiler_params=_params("parallel", "parallel"),
    )(cc, w_ada, b_ada.reshape(depth, 1, n))


def _modulated_norm(x, nw, shift, scale):
    ms = jnp.mean(x * x, axis=-1, keepdims=True)
    return (x * lax.rsqrt(ms + NORM_EPS) * nw) * (1.0 + scale) + shift


def _norm_kernel(x_ref, nw_ref, mod_ref, o_ref, *, n_lat_tiles):
    is_ctx = pl.program_id(0) >= n_lat_tiles
    d = x_ref.shape[1]
    shift = jnp.where(is_ctx, mod_ref[1:2, 0:d], mod_ref[0:1, 0:d])
    scale = jnp.where(is_ctx, mod_ref[1:2, d:2 * d], mod_ref[0:1, d:2 * d])
    o_ref[...] = _modulated_norm(x_ref[...], nw_ref[...], shift, scale).astype(o_ref.dtype)


def _norm_mod(xcat, nw, mod, n_lat):
    t, d = xcat.shape
    tm = 256
    return pl.pallas_call(
        functools.partial(_norm_kernel, n_lat_tiles=n_lat // tm),
        grid=(t // tm,),
        in_specs=[pl.BlockSpec((tm, d), lambda i: (i, 0)),
                  pl.BlockSpec((1, d), lambda i: (0, 0)),
                  pl.BlockSpec((8, 2 * d), lambda i: (0, 0))],
        out_specs=pl.BlockSpec((tm, d), lambda i: (i, 0)),
        out_shape=jax.ShapeDtypeStruct((t, d), BF16),
        compiler_params=_params("parallel"),
    )(xcat, nw, mod)


def _mm_kernel(x_ref, w_ref, o_ref):
    o_ref[...] = jnp.dot(x_ref[...], w_ref[0], preferred_element_type=F32).astype(o_ref.dtype)


def _mm_t_kernel(x_ref, w_ref, o_ref, acc_ref):
    acc_ref[...] = jnp.dot(x_ref[...], w_ref[0], preferred_element_type=F32)
    o_ref[...] = acc_ref[...].T.astype(o_ref.dtype)


def _mm_rope_kernel(x_ref, w_ref, cos_ref, sin_ref, o_ref, *, q_tiles):
    scale = jnp.where(pl.program_id(1) < q_tiles, Q_SCALE, 1.0)
    t = jnp.dot(x_ref[...], w_ref[0], preferred_element_type=F32) * scale
    c = cos_ref[...]
    s = sin_ref[...]
    lane = lax.broadcasted_iota(jnp.int32, c.shape, 1)
    first = (lane % A_HEAD_DIM) < (A_HEAD_DIM // 2)
    for a in range(0, t.shape[1], LANES):
        th = t[:, a:a + LANES]
        sw = jnp.where(first, pltpu.roll(th, LANES - A_HEAD_DIM // 2, 1), pltpu.roll(th, A_HEAD_DIM // 2, 1))
        o_ref[:, a:a + LANES] = (th * c + sw * s).astype(o_ref.dtype)


def _proj(xn, w, layer, col_segments, out_dtype, tn, rope_q_cols=None, rope=None, transposed=False):
    t, d = xn.shape
    tm = _tile(t, 1408, 256)
    assert all(c0 % tn == 0 and width % tn == 0 for c0, width in col_segments)
    n_tiles = sum(width // tn for _, width in col_segments)
    scratch = []
    bounds, offs, acc = [], [], 0
    for c0, width in col_segments:
        offs.append(c0 // tn - acc)
        acc += width // tn
        bounds.append(acc)

    def wcol(j):
        col = j + offs[-1]
        for bnd, off in zip(reversed(bounds[:-1]), reversed(offs[:-1])):
            col = jnp.where(j < bnd, j + off, col)
        return col

    in_specs = [pl.BlockSpec((tm, d), lambda i, j: (i, 0)),
                pl.BlockSpec((1, d, tn), lambda i, j: (layer, 0, wcol(j)))]
    args = [xn, w]
    kern = _mm_kernel
    out_spec = pl.BlockSpec((tm, tn), lambda i, j: (i, j))
    out_shape = (t, n_tiles * tn)
    if rope is not None:
        in_specs += [pl.BlockSpec((tm, LANES), lambda i, j: (i, 0))] * 2
        args += list(rope)
        kern = functools.partial(_mm_rope_kernel, q_tiles=rope_q_cols // tn)
    if transposed:
        kern = _mm_t_kernel
        out_spec = pl.BlockSpec((tn, tm), lambda i, j: (j, i))
        out_shape = (n_tiles * tn, t)
        scratch = [pltpu.VMEM((tm, tn), F32)]
    return pl.pallas_call(
        kern,
        grid=(t // tm, n_tiles),
        in_specs=in_specs,
        out_specs=out_spec,
        out_shape=jax.ShapeDtypeStruct(out_shape, out_dtype),
        scratch_shapes=scratch,
        compiler_params=_params("parallel", "parallel"),
    )(*args)


def _attn_kernel(q_ref, k_ref, vt_ref, z_ref, lam_ref, sw_ref, o_ref, *, tk, lam_init):
    tq = q_ref.shape[0]
    nkeys = k_ref.shape[0]
    q = q_ref[...]
    lane = lax.broadcasted_iota(jnp.int32, q.shape, 1)
    zero = jnp.zeros_like(q)
    qz = jnp.concatenate([jnp.where(lane < A_HEAD_DIM, q, zero),
                          jnp.where(lane >= A_HEAD_DIM, q, zero)], axis=0)
    m = jnp.full((1, 2 * tq), NEG_BIG, F32)
    acc = jnp.zeros((V_ROWS, 2 * tq), F32)
    tail_row = lax.broadcasted_iota(jnp.int32, (V_ROWS - A_VAL_DIM, tk), 0)
    tail = jnp.where(tail_row == 0, 1.0, 0.0).astype(BF16)

    def scores(c0):
        return lax.dot_general(k_ref[c0:c0 + tk, :], qz, (((1,), (1,)), ((), ())),
                               preferred_element_type=F32)

    st_next = scores(0)
    for c0 in range(0, nkeys, tk):
        st = st_next
        if c0 + tk < nkeys:
            st_next = scores(c0 + tk)
        m_new = jnp.maximum(m, jnp.max(st, axis=0, keepdims=True))
        alpha = jnp.exp2(m - m_new)
        p = jnp.exp2(st - m_new).astype(BF16)
        v_ext = jnp.concatenate([vt_ref[:, c0:c0 + tk], tail], axis=0)
        acc = alpha * acc + jnp.dot(v_ext, p, preferred_element_type=F32)
        m = m_new
    o1 = acc[0:A_VAL_DIM, 0:tq] / acc[A_VAL_DIM:A_VAL_DIM + 1, 0:tq]
    o2 = acc[0:A_VAL_DIM, tq:] / acc[A_VAL_DIM:A_VAL_DIM + 1, tq:]
    lp = lam_ref[...]
    lam = (jnp.exp(jnp.sum(lp[0:1] * lp[1:2], axis=1, keepdims=True))
           - jnp.exp(jnp.sum(lp[2:3] * lp[3:4], axis=1, keepdims=True)) + lam_init)
    o = (o1 - lam * o2).T
    ms = jnp.mean(o * o, axis=-1, keepdims=True)
    y = (o * lax.rsqrt(ms + SUBLN_EPS) * sw_ref[...]) * (1.0 - lam_init)
    o_ref[...] = (y * _silu(z_ref[...].astype(F32))).astype(o_ref.dtype)


def _attention(qk, zz, vt, lam_p, subln_w, lam_init, *, q_row0, n_q, k_row0, n_keys, tq, tk):
    qb0 = q_row0 // tq
    kb0 = k_row0 // n_keys
    return pl.pallas_call(
        functools.partial(_attn_kernel, tk=tk, lam_init=lam_init),
        grid=(A_HEADS, n_q // tq),
        in_specs=[pl.BlockSpec((tq, LANES), lambda h, i: (qb0 + i, h)),
                  pl.BlockSpec((n_keys, LANES), lambda h, i: (kb0, A_HEADS + h)),
                  pl.BlockSpec((A_VAL_DIM, n_keys), lambda h, i: (h, kb0)),
                  pl.BlockSpec((tq, LANES), lambda h, i: (qb0 + i, h)),
                  pl.BlockSpec(lam_p.shape, lambda h, i: (0, 0)),
                  pl.BlockSpec((1, A_VAL_DIM), lambda h, i: (0, 0))],
        out_specs=pl.BlockSpec((tq, LANES), lambda h, i: (i, h)),
        out_shape=jax.ShapeDtypeStruct((n_q, V_A), BF16),
        compiler_params=_params("parallel", "parallel"),
    )(qk, qk, vt, zz, lam_p, subln_w)


def _conv_kernel(x_ref, w_ref, o_ref, pad_ref, *, n_lat, n_ctx, rows):
    j = pl.program_id(0)
    half = (CONV_K - 1) // 2
    lat0 = 8
    ctx0 = lat0 + n_lat + 16
    zeros8 = jnp.zeros((8, LANES), F32)
    pad_ref[0:8, :] = zeros8
    pad_ref[lat0 + n_lat:lat0 + n_lat + 8, :] = zeros8
    pad_ref[lat0 + n_lat + 8:ctx0, :] = zeros8
    pad_ref[ctx0 + n_ctx:ctx0 + n_ctx + 8, :] = zeros8
    pad_ref[lat0:lat0 + n_lat, :] = x_ref[0:n_lat, :].astype(F32)
    pad_ref[ctx0:ctx0 + n_ctx, :] = x_ref[n_lat:n_lat + n_ctx, :].astype(F32)
    w = w_ref[...]
    is_qk = j < 2 * B_HEADS
    post = jnp.where(j < B_HEADS, B_DK ** -0.5, 1.0)
    for src0, dst0, n in ((lat0, 0, n_lat), (ctx0, n_lat, n_ctx)):
        for r in range(0, n, rows):
            acc = jnp.zeros((rows, LANES), F32)
            for tap in range(CONV_K):
                a = src0 + r + tap - half
                acc = acc + pad_ref[a:a + rows, :] * w[tap:tap + 1, :]
            y = _silu(acc)
            nrm = lax.rsqrt(jnp.sum(y * y, axis=-1, keepdims=True) + L2_EPS) * post
            y = y * jnp.where(is_qk, nrm, 1.0)
            o_ref[dst0 + r:dst0 + r + rows, :] = y.astype(o_ref.dtype)


def _conv_prep(pb, conv_w, n_lat, n_ctx):
    t = pb.shape[0]
    nblk = CONV_CH // LANES
    return pl.pallas_call(
        functools.partial(_conv_kernel, n_lat=n_lat, n_ctx=n_ctx, rows=256),
        grid=(nblk,),
        in_specs=[pl.BlockSpec((t, LANES), lambda j: (0, j)),
                  pl.BlockSpec((CONV_K, LANES), lambda j: (0, j))],
        out_specs=pl.BlockSpec((t, LANES), lambda j: (0, j)),
        out_shape=jax.ShapeDtypeStruct((t, CONV_CH), BF16),
        scratch_shapes=[pltpu.VMEM((t + 40, LANES), F32)],
        compiler_params=_params("parallel"),
    )(pb, conv_w)


def _gates_kernel(ab_ref, alog_ref, dtb_ref, o_ref):
    c = DELTA_CHUNK
    nh = B_HEADS
    r = lax.broadcasted_iota(jnp.int32, (c, c), 0)
    l = lax.broadcasted_iota(jnp.int32, (c, c), 1)
    upper = (r <= l).astype(F32)
    lower = (r >= l).astype(F32)
    neg_a = -jnp.exp(alog_ref[...])
    dtb = dtb_ref[...]
    for t0 in range(0, ab_ref.shape[1], c):
        x = ab_ref[0:2 * nh, t0:t0 + c] + dtb
        sp = jnp.maximum(x, 0.0) + jnp.log1p(jnp.exp(-jnp.abs(x)))
        g = neg_a * sp
        o_ref[0:nh, t0:t0 + c] = jnp.dot(g[0:nh], upper, preferred_element_type=F32,
                                          precision=lax.Precision.HIGHEST)
        o_ref[nh:2 * nh, t0:t0 + c] = jnp.dot(g[nh:2 * nh], lower, preferred_element_type=F32,
                                               precision=lax.Precision.HIGHEST)
        o_ref[2 * nh:4 * nh, t0:t0 + c] = jax.nn.sigmoid(ab_ref[2 * nh:4 * nh, t0:t0 + c])


def _gates(ab_t, alog_b, dtb_b):
    rows, t = ab_t.shape
    tl = _tile(t, 1408, 256)
    return pl.pallas_call(
        _gates_kernel,
        grid=(t // tl,),
        in_specs=[pl.BlockSpec((rows, tl), lambda i: (0, i)),
                  pl.BlockSpec(alog_b.shape, lambda i: (0, 0)),
                  pl.BlockSpec(dtb_b.shape, lambda i: (0, 0))],
        out_specs=pl.BlockSpec((rows, tl), lambda i: (0, i)),
        out_shape=jax.ShapeDtypeStruct((rows, t), F32),
        compiler_params=_params("parallel"),
    )(ab_t, alog_b, dtb_b)


def _delta_prep_kernel(q_ref, k_ref, v_ref, g_ref, aq_ref, bo_ref, e_ref, *, n_chunks):
    c = DELTA_CHUNK
    row = lax.broadcasted_iota(jnp.int32, (c, c), 0)
    col = lax.broadcasted_iota(jnp.int32, (c, c), 1)
    eye = row == col
    eye_f = eye.astype(F32)
    n_levels = int(math.log2(c))
    same_blk = [(row >> s) == (col >> s) for s in range(n_levels + 1)]
    off_blk = [None] + [jnp.logical_and(same_blk[lv + 1], jnp.logical_not(same_blk[lv]))
                        for lv in range(1, n_levels)]
    nt = (((1,), (1,)), ((), ()))

    def to_col(r):
        return jnp.sum(jnp.where(eye, jnp.broadcast_to(r, (c, c)), 0.0), axis=1, keepdims=True)

    chains = []
    for ci in range(n_chunks):
        q = q_ref[ci * c:(ci + 1) * c, :]
        k = k_ref[ci * c:(ci + 1) * c, :]
        v = v_ref[ci * c:(ci + 1) * c, :]
        rows = g_ref[0, ci]
        kk = lax.dot_general(k, k, nt, preferred_element_type=F32)
        qk = lax.dot_general(q, k, nt, preferred_element_type=F32)
        for d in range(2):
            if d == 0:
                g_r, b_r = rows[0:1], rows[2:3]
                g_last = g_r[:, c - 1:c]
                mask, strict = row >= col, row > col
            else:
                g_r, b_r = rows[1:2], rows[3:4]
                g_last = g_r[:, 0:1]
                mask, strict = row <= col, row < col
            g_c = to_col(g_r)
            b_c = to_col(b_r)
            decay = jnp.where(mask, jnp.exp(jnp.where(mask, g_c - g_r, 0.0)), 0.0)
            n_mat = jnp.where(strict, kk * b_c * decay, 0.0)
            chains.append(dict(ci=ci, d=d, q=q, k=k, v=v, qk=qk, g_c=g_c, b_c=b_c, g_last=g_last, mask=mask,
                               decay=decay, n_mat=n_mat, x=eye_f - jnp.where(same_blk[1], n_mat, 0.0)))
    for lv in range(1, n_levels):
        for ch in chains:
            ch["xb"] = ch["x"].astype(BF16)
            ch["a"] = jnp.dot(ch["xb"], jnp.where(off_blk[lv], ch["n_mat"], 0.0).astype(BF16),
                              preferred_element_type=F32)
        for ch in chains:
            ch["x"] = ch["x"] - jnp.dot(ch["a"].astype(BF16), ch["xb"], preferred_element_type=F32)
    for ch in chains:
        kf = ch["k"].astype(F32)
        ch["e_g"] = jnp.exp(ch["g_c"])
        kb = kf * ch["b_c"]
        rhs = jnp.concatenate([(kb * ch["e_g"]).astype(BF16),
                               (ch["v"].astype(F32) * ch["b_c"]).astype(BF16)], axis=1)
        ch["wu"] = jnp.dot(ch["x"].astype(BF16), rhs, preferred_element_type=F32).astype(BF16)
        ch["attn"] = jnp.where(ch["mask"], ch["qk"] * ch["decay"], 0.0).astype(BF16)
        ch["ke_t"] = (kf * jnp.exp(ch["g_last"] - ch["g_c"])).T.astype(BF16)
    for ch in chains:
        ch["m1"] = jnp.dot(ch["ke_t"], ch["wu"], preferred_element_type=F32)
        ch["m2"] = jnp.dot(ch["attn"], ch["wu"], preferred_element_type=F32)
    for ch in chains:
        ci, d, m1, m2 = ch["ci"], ch["d"], ch["m1"], ch["m2"]
        aq_ref[d, 0, ci, 0:c, :] = (-m1[:, 0:B_DV]).astype(BF16)
        aq_ref[d, 0, ci, c:2 * c, :] = (ch["q"].astype(F32) * ch["e_g"] - m2[:, 0:B_DV]).astype(BF16)
        bo_ref[d, 0, ci, 0:c, :] = m1[:, B_DV:].astype(BF16)
        bo_ref[d, 0, ci, c:2 * c, :] = m2[:, B_DV:].astype(BF16)
    for ci in range(n_chunks):
        e_f, e_b = (jnp.broadcast_to(jnp.exp(chains[2 * ci + d]["g_last"]), (1, c)) for d in range(2))
        e_ref[0, ci] = jnp.concatenate([e_f, e_b, jnp.zeros((6, c), F32)], axis=0)


def _delta_prep(qkv, g4):
    t = qkv.shape[0]
    c = DELTA_CHUNK
    nch = t // c
    g = 6
    rows = g * c
    return pl.pallas_call(
        functools.partial(_delta_prep_kernel, n_chunks=g),
        grid=(B_HEADS, nch // g),
        in_specs=[pl.BlockSpec((rows, LANES), lambda h, i: (i, h)),
                  pl.BlockSpec((rows, LANES), lambda h, i: (i, B_HEADS + h)),
                  pl.BlockSpec((rows, LANES), lambda h, i: (i, 2 * B_HEADS + h)),
                  pl.BlockSpec((1, g, 8, c), lambda h, i: (h, i, 0, 0))],
        out_specs=[pl.BlockSpec((2, 1, g, 2 * c, B_DV), lambda h, i: (0, h, i, 0, 0)),
                   pl.BlockSpec((2, 1, g, 2 * c, B_DV), lambda h, i: (0, h, i, 0, 0)),
                   pl.BlockSpec((1, g, 8, c), lambda h, i: (h, i, 0, 0))],
        out_shape=[jax.ShapeDtypeStruct((2, B_HEADS, nch, 2 * c, B_DV), BF16),
                   jax.ShapeDtypeStruct((2, B_HEADS, nch, 2 * c, B_DV), BF16),
                   jax.ShapeDtypeStruct((B_HEADS, nch, 8, c), F32)],
        compiler_params=_params("parallel", "parallel"),
    )(qkv, qkv, qkv, g4)


def _delta_scan_kernel(aqf_ref, bof_ref, ef_ref, aqb_ref, bob_ref, eb_ref, of_ref, ob_ref, st_ref):
    c = DELTA_CHUNK

    @pl.when(pl.program_id(0) == 0)
    def _():
        st_ref[...] = jnp.zeros_like(st_ref)

    for d, (aq_ref, bo_ref, e_ref, o_ref) in enumerate(((aqf_ref, bof_ref, ef_ref, of_ref),
                                                        (aqb_ref, bob_ref, eb_ref, ob_ref))):
        for h in range(B_HEADS):
            state = st_ref[d, h]
            res = jnp.dot(aq_ref[0, h, 0], state.astype(BF16), preferred_element_type=F32)
            bo = bo_ref[0, h, 0].astype(F32)
            o_ref[:, h * B_DV:(h + 1) * B_DV] = res[c:] + bo[c:]
            st_ref[d, h] = state * e_ref[h, 0, d:d + 1, 0:1] + res[0:c] + bo[0:c]


def _delta_scan(aq, bo, ee, n_lat, n_ctx):
    c = DELTA_CHUNK
    nch = aq.shape[2]
    n_lat_c = n_lat // c
    n_ctx_c = n_ctx // c
    t = nch * c

    def fwd(s):
        return jnp.where(s < n_ctx_c, s + n_lat_c, s - n_ctx_c)

    def bwd(s):
        return nch - 1 - s

    def big(direction, order):
        return pl.BlockSpec((1, B_HEADS, 1, 2 * c, B_DV), lambda s: (direction, 0, order(s), 0, 0))

    def small(order):
        return pl.BlockSpec((B_HEADS, 1, 8, c), lambda s: (0, order(s), 0, 0))

    return pl.pallas_call(
        _delta_scan_kernel,
        grid=(nch,),
        in_specs=[big(0, fwd), big(0, fwd), small(fwd), big(1, bwd), big(1, bwd), small(bwd)],
        out_specs=[pl.BlockSpec((c, V_B), lambda s: (fwd(s), 0)),
                   pl.BlockSpec((c, V_B), lambda s: (bwd(s), 0))],
        out_shape=[jax.ShapeDtypeStruct((t, V_B), F32), jax.ShapeDtypeStruct((t, V_B), F32)],
        scratch_shapes=[pltpu.VMEM((2, B_HEADS, B_DK, B_DV), F32)],
        compiler_params=_params("arbitrary"),
    )(aq, bo, ee, aq, bo, ee)


def _delta_gated(of_ref, ob_ref, z_ref, nw_ref):
    nw = nw_ref[...]
    heads = []
    for h in range(B_HEADS):
        sl = slice(h * B_DV, (h + 1) * B_DV)
        o = of_ref[:, sl] + ob_ref[:, sl]
        ms = jnp.mean(o * o, axis=-1, keepdims=True)
        y = (o * lax.rsqrt(ms + NORM_EPS) * nw) * _silu(z_ref[:, sl].astype(F32))
        heads.append(y.astype(BF16))
    return jnp.concatenate(heads, axis=1)


def _out_kernel(ya_ref, yac_ref, of_ref, ob_ref, z_ref, dnw_ref, wa_ref, wb_ref, x_ref, mod_ref, nw_ref, modn_ref,
                o_ref, xn_ref, *, n_lat_tiles):
    is_ctx = pl.program_id(0) >= n_lat_tiles
    d = x_ref.shape[1]
    gate = jnp.where(is_ctx, mod_ref[1:2, :], mod_ref[0:1, :])
    ya = jnp.where(is_ctx, yac_ref[...], ya_ref[...])
    acc = jnp.dot(ya, wa_ref[0], preferred_element_type=F32)
    acc = acc + jnp.dot(_delta_gated(of_ref, ob_ref, z_ref, dnw_ref), wb_ref[0], preferred_element_type=F32)
    x = x_ref[...] + gate * acc
    o_ref[...] = x
    shift = jnp.where(is_ctx, modn_ref[1:2, 0:d], modn_ref[0:1, 0:d])
    scale = jnp.where(is_ctx, modn_ref[1:2, d:2 * d], modn_ref[0:1, d:2 * d])
    xn_ref[...] = _modulated_norm(x, nw_ref[...], shift, scale).astype(xn_ref.dtype)


def _out_final_kernel(ya_ref, of_ref, ob_ref, z_ref, dnw_ref, wa_ref, wb_ref, x_ref, mod_ref, fw_ref, o_ref):
    acc = jnp.dot(ya_ref[...], wa_ref[0], preferred_element_type=F32)
    acc = acc + jnp.dot(_delta_gated(of_ref, ob_ref, z_ref, dnw_ref), wb_ref[0], preferred_element_type=F32)
    x = x_ref[...] + mod_ref[0:1, :] * acc
    ms = jnp.mean(x * x, axis=-1, keepdims=True)
    o_ref[...] = x * lax.rsqrt(ms + NORM_EPS) * fw_ref[...]


def _delta_out_specs(tm):
    return [pl.BlockSpec((tm, V_B), lambda i: (i, 0)),
            pl.BlockSpec((tm, V_B), lambda i: (i, 0)),
            pl.BlockSpec((tm, V_B), lambda i: (i, V_A // V_B)),
            pl.BlockSpec((1, B_DV), lambda i: (0, 0))]


def _out_proj(ya, ya_ctx, out_f, out_b, zz, dn_norm_w, w_out, layer, xcat, mod, nw_next, mod_next, n_lat):
    t, d = xcat.shape
    tm = 256
    n_lat_tiles = n_lat // tm
    return pl.pallas_call(
        functools.partial(_out_kernel, n_lat_tiles=n_lat_tiles),
        grid=(t // tm,),
        in_specs=[pl.BlockSpec((tm, V_A), lambda i: (jnp.minimum(i, n_lat_tiles - 1), 0)),
                  pl.BlockSpec((tm, V_A), lambda i: (jnp.maximum(i - n_lat_tiles, 0), 0)),
                  *_delta_out_specs(tm),
                  pl.BlockSpec((1, V_A, d), lambda i: (layer, 0, 0)),
                  pl.BlockSpec((1, V_B, d), lambda i: (layer, V_A // V_B, 0)),
                  pl.BlockSpec((tm, d), lambda i: (i, 0)),
                  pl.BlockSpec((8, d), lambda i: (0, 2)),
                  pl.BlockSpec((1, d), lambda i: (0, 0)),
                  pl.BlockSpec((8, 2 * d), lambda i: (0, 0))],
        out_specs=[pl.BlockSpec((tm, d), lambda i: (i, 0)),
                   pl.BlockSpec((tm, d), lambda i: (i, 0))],
        out_shape=[jax.ShapeDtypeStruct((t, d), F32), jax.ShapeDtypeStruct((t, d), BF16)],
        compiler_params=_params("parallel"),
    )(ya, ya_ctx, out_f, out_b, zz, dn_norm_w, w_out, w_out, xcat, mod, nw_next, mod_next)


def _out_proj_final(ya, out_f, out_b, zz, dn_norm_w, w_out, layer, xcat, mod, final_w, n_lat):
    d = xcat.shape[1]
    tm = 256
    return pl.pallas_call(
        _out_final_kernel,
        grid=(n_lat // tm,),
        in_specs=[pl.BlockSpec((tm, V_A), lambda i: (i, 0)),
                  *_delta_out_specs(tm),
                  pl.BlockSpec((1, V_A, d), lambda i: (layer, 0, 0)),
                  pl.BlockSpec((1, V_B, d), lambda i: (layer, V_A // V_B, 0)),
                  pl.BlockSpec((tm, d), lambda i: (i, 0)),
                  pl.BlockSpec((8, d), lambda i: (0, 2)),
                  pl.BlockSpec((1, d), lambda i: (0, 0))],
        out_specs=pl.BlockSpec((tm, d), lambda i: (i, 0)),
        out_shape=jax.ShapeDtypeStruct((n_lat, d), F32),
        compiler_params=_params("parallel"),
    )(ya, out_f, out_b, zz, dn_norm_w, w_out, w_out, xcat, mod, final_w)


def _rope_tables(n_lat, n_ctx):
    n_rows = n_lat // GRID_W
    rows = jnp.repeat(jnp.arange(n_rows, dtype=F32), GRID_W)
    cols = jnp.tile(jnp.arange(GRID_W, dtype=F32), n_rows)
    n_freq = A_HEAD_DIM // 4
    inv = ROPE_THETA ** (-jnp.arange(n_freq, dtype=F32) / n_freq)
    ang = jnp.concatenate([rows[:, None] * inv, cols[:, None] * inv], axis=-1)
    cos = jnp.cos(ang)
    sin = jnp.sin(ang)
    cos_t = jnp.tile(cos, (1, LANES // (A_HEAD_DIM // 2)))
    sin_t = jnp.tile(jnp.concatenate([-sin, sin], axis=-1), (1, LANES // A_HEAD_DIM))
    cos_t = jnp.concatenate([cos_t, jnp.ones((n_ctx, LANES), F32)], axis=0)
    sin_t = jnp.concatenate([sin_t, jnp.zeros((n_ctx, LANES), F32)], axis=0)
    return cos_t, sin_t


def kernel(x, c, ctx, c_ctx, w_ada, b_ada, norm_w, w_in, conv_w, dn_a_log, dn_dt_bias, dn_norm_w,
           diff_lambda, subln_w, w_out, final_norm_w):
    depth = w_ada.shape[0]
    n_lat = x.shape[1]
    n_ctx = ctx.shape[1]
    t = n_lat + n_ctx
    d = x.shape[2]
    assert x.shape[0] == 1 and d == D_MODEL
    assert n_lat % 512 == 0 and n_ctx == 256 and t % 768 == 0

    xcat = jnp.concatenate([x[0], ctx[0]], axis=0)
    cc = jnp.zeros((8, d), F32).at[0].set(c[0]).at[1].set(c_ctx)
    mods = _ada_all(cc, w_ada, b_ada)
    cos_t, sin_t = _rope_tables(n_lat, n_ctx)

    o_q, o_v, o_z, o_b, o_zb, o_ab = 0, 2 * QK_A, 2 * QK_A + V_A, 2 * QK_A + 2 * V_A, \
        2 * QK_A + 2 * V_A + CONV_CH, 2 * QK_A + 2 * V_A + CONV_CH + V_B
    w_in_bf = w_in[:, :, :o_ab].astype(BF16)
    w_out_bf = w_out.astype(BF16)
    w_ab = jnp.pad(w_in[:, :, o_ab:], ((0, 0), (0, 0), (0, LANES - 4 * B_HEADS))).astype(BF16)

    nch = t // DELTA_CHUNK
    xn = _norm_mod(xcat, norm_w[0][None, :], mods[0], n_lat)
    for i in range(depth):
        last = i == depth - 1
        lam_init = 0.8 - 0.6 * math.exp(-0.3 * i)
        mod = mods[i]
        qk = _proj(xn, w_in_bf, i, [(o_q, 2 * QK_A)], BF16, 256, rope_q_cols=QK_A, rope=(cos_t, sin_t))
        vt = _proj(xn, w_in_bf, i, [(o_v, V_A)], BF16, 512, transposed=True)
        zz = _proj(xn, w_in_bf, i, [(o_z, V_A), (o_zb, V_B)], BF16, 512)
        pb = _proj(xn, w_in_bf, i, [(o_b, CONV_CH)], BF16, 512)
        ab = _proj(xn, w_ab, i, [(0, LANES)], F32, LANES)

        lam_p = diff_lambda[i]
        sw = subln_w[i][None, :]
        ya = _attention(qk, zz, vt, lam_p, sw, lam_init, q_row0=0, n_q=n_lat, k_row0=0, n_keys=t,
                        tq=_tile(n_lat, 1024, 512), tk=768)
        if not last:
            ya_ctx = _attention(qk, zz, vt, lam_p, sw, lam_init, q_row0=n_lat, n_q=n_ctx, k_row0=n_lat,
                                n_keys=n_ctx, tq=256, tk=256)

        qkv = _conv_prep(pb, conv_w[i], n_lat, n_ctx)
        ab_t = ab[:, 0:4 * B_HEADS].T
        alog_b = jnp.broadcast_to(dn_a_log[i].reshape(2 * B_HEADS, 1), (2 * B_HEADS, DELTA_CHUNK))
        dtb_b = jnp.broadcast_to(dn_dt_bias[i].reshape(2 * B_HEADS, 1), (2 * B_HEADS, DELTA_CHUNK))
        gates = _gates(ab_t, alog_b, dtb_b)
        g4 = gates.reshape(4, B_HEADS, nch, DELTA_CHUNK).transpose(1, 2, 0, 3)
        g4 = jnp.concatenate([g4, jnp.zeros_like(g4)], axis=2)
        aq, bo, ee = _delta_prep(qkv, g4)
        out_f, out_b = _delta_scan(aq, bo, ee, n_lat, n_ctx)
        dnw = dn_norm_w[i][None, :]

        if not last:
            xcat, xn = _out_proj(ya, ya_ctx, out_f, out_b, zz, dnw, w_out_bf, i, xcat, mod,
                                 norm_w[i + 1][None, :], mods[i + 1], n_lat)
        else:
            out = _out_proj_final(ya, out_f, out_b, zz, dnw, w_out_bf, i, xcat, mod, final_norm_w[None, :], n_lat)
    return out[None]
```

```python
import functools
import math

import jax
import jax.numpy as jnp
from jax import lax
from jax.experimental import pallas as pl
from jax.experimental.pallas import tpu as pltpu

F32 = jnp.float32
BF16 = jnp.bfloat16

LANES = 128
VMEM_LIMIT_BYTES = 56 * 1024 * 1024

D_MODEL = 2048
GRID_W = 64
A_HEADS = 8
A_HEAD_DIM = 64
A_VAL_DIM = 2 * A_HEAD_DIM
QK_A = A_HEADS * 2 * A_HEAD_DIM
V_A = A_HEADS * A_VAL_DIM
B_HEADS = 8
B_DK = 128
B_DV = 128
QK_B = B_HEADS * B_DK
V_B = B_HEADS * B_DV
CONV_K = 5
CONV_CH = 2 * QK_B + V_B
ROPE_THETA = 10000.0
NORM_EPS = 1e-6
SUBLN_EPS = 1e-5
L2_EPS = 1e-6

DELTA_CHUNK = 128
V_ROWS = A_VAL_DIM + 16
NEG_BIG = -1e30
Q_SCALE = A_HEAD_DIM ** -0.5 * math.log2(math.e)


def _params(*sem, flags=None):
    return pltpu.CompilerParams(dimension_semantics=sem, vmem_limit_bytes=VMEM_LIMIT_BYTES, flags=flags)


def _silu(x):
    return x * jax.nn.sigmoid(x)


def _tile(n, preferred, fallback):
    return preferred if n % preferred == 0 else fallback


def _ada_kernel(c_ref, w_ref, b_ref, o_ref):
    s = _silu(c_ref[...])
    o_ref[0] = jnp.dot(s.astype(BF16), w_ref[0].astype(BF16), preferred_element_type=F32) + b_ref[0]


def _ada_all(cc, w_ada, b_ada):
    depth, d, n = w_ada.shape
    tn = 512
    return pl.pallas_call(
        _ada_kernel,
        grid=(depth, n // tn),
        in_specs=[pl.BlockSpec((8, d), lambda l, j: (0, 0)),
                  pl.BlockSpec((1, d, tn), lambda l, j: (l, 0, j)),
                  pl.BlockSpec((1, 1, tn), lambda l, j: (l, 0, j))],
        out_specs=pl.BlockSpec((1, 8, tn), lambda l, j: (l, 0, j)),
        out_shape=jax.ShapeDtypeStruct((depth, 8, n), F32),
        compiler_params=_params("parallel", "parallel"),
    )(cc, w_ada, b_ada.reshape(depth, 1, n))


def _modulated_norm(x, nw, shift, scale):
    ms = jnp.mean(x * x, axis=-1, keepdims=True)
    return (x * lax.rsqrt(ms + NORM_EPS) * nw) * (1.0 + scale) + shift


def _norm_kernel(x_ref, nw_ref, mod_ref, o_ref, *, n_lat_tiles):
    is_ctx = pl.program_id(0) >= n_lat_tiles
    d = x_ref.shape[1]
    shift = jnp.where(is_ctx, mod_ref[1:2, 0:d], mod_ref[0:1, 0:d])
    scale = jnp.where(is_ctx, mod_ref[1:2, d:2 * d], mod_ref[0:1, d:2 * d])
    o_ref[...] = _modulated_norm(x_ref[...], nw_ref[...], shift, scale).astype(o_ref.dtype)


def _norm_mod(xcat, nw, mod, n_lat):
    t, d = xcat.shape
    tm = 256
    return pl.pallas_call(
        functools.partial(_norm_kernel, n_lat_tiles=n_lat // tm),
        grid=(t // tm,),
        in_specs=[pl.BlockSpec((tm, d), lambda i: (i, 0)),
                  pl.BlockSpec((1, d), lambda i: (0, 0)),
                  pl.BlockSpec((8, 2 * d), lambda i: (0, 0))],
        out_specs=pl.BlockSpec((tm, d), lambda i: (i, 0)),
        out_shape=jax.ShapeDtypeStruct((t, d), BF16),
        compiler_params=_params("parallel"),
    )(xcat, nw, mod)


def _mm_kernel(x_ref, w_ref, o_ref):
    o_ref[...] = jnp.dot(x_ref[...], w_ref[0], preferred_element_type=F32).astype(o_ref.dtype)


def _mm_t_kernel(x_ref, w_ref, o_ref, acc_ref):
    acc_ref[...] = jnp.dot(x_ref[...], w_ref[0], preferred_element_type=F32)
    o_ref[...] = acc_ref[...].T.astype(o_ref.dtype)


def _mm_rope_kernel(x_ref, w_ref, cos_ref, sin_ref, o_ref, *, q_tiles):
    scale = jnp.where(pl.program_id(1) < q_tiles, Q_SCALE, 1.0)
    t = jnp.dot(x_ref[...], w_ref[0], preferred_element_type=F32) * scale
    c = cos_ref[...]
    s = sin_ref[...]
    lane = lax.broadcasted_iota(jnp.int32, c.shape, 1)
    first = (lane % A_HEAD_DIM) < (A_HEAD_DIM // 2)
    for a in range(0, t.shape[1], LANES):
        th = t[:, a:a + LANES]
        sw = jnp.where(first, pltpu.roll(th, LANES - A_HEAD_DIM // 2, 1), pltpu.roll(th, A_HEAD_DIM // 2, 1))
        o_ref[:, a:a + LANES] = (th * c + sw * s).astype(o_ref.dtype)


def _proj(xn, w, layer, col_segments, out_dtype, tn, rope_q_cols=None, rope=None, transposed=False):
    t, d = xn.shape
    tm = _tile(t, 1408, 256)
    assert all(c0 % tn == 0 and width % tn == 0 for c0, width in col_segments)
    n_tiles = sum(width // tn for _, width in col_segments)
    scratch = []
    bounds, offs, acc = [], [], 0
    for c0, width in col_segments:
        offs.append(c0 // tn - acc)
        acc += width // tn
        bounds.append(acc)

    def wcol(j):
        col = j + offs[-1]
        for bnd, off in zip(reversed(bounds[:-1]), reversed(offs[:-1])):
            col = jnp.where(j < bnd, j + off, col)
        return col

    in_specs = [pl.BlockSpec((tm, d), lambda i, j: (i, 0)),
                pl.BlockSpec((1, d, tn), lambda i, j: (layer, 0, wcol(j)))]
    args = [xn, w]
    kern = _mm_kernel
    out_spec = pl.BlockSpec((tm, tn), lambda i, j: (i, j))
    out_shape = (t, n_tiles * tn)
    if rope is not None:
        in_specs += [pl.BlockSpec((tm, LANES), lambda i, j: (i, 0))] * 2
        args += list(rope)
        kern = functools.partial(_mm_rope_kernel, q_tiles=rope_q_cols // tn)
    if transposed:
        kern = _mm_t_kernel
        out_spec = pl.BlockSpec((tn, tm), lambda i, j: (j, i))
        out_shape = (n_tiles * tn, t)
        scratch = [pltpu.VMEM((tm, tn), F32)]
    return pl.pallas_call(
        kern,
        grid=(t // tm, n_tiles),
        in_specs=in_specs,
        out_specs=out_spec,
        out_shape=jax.ShapeDtypeStruct(out_shape, out_dtype),
        scratch_shapes=scratch,
        compiler_params=_params("parallel", "parallel"),
    )(*args)


def _attn_kernel(q_ref, k_ref, vt_ref, z_ref, lam_ref, sw_ref, o_ref, *, tk, lam_init):
    tq = q_ref.shape[0]
    nkeys = k_ref.shape[0]
    q = q_ref[...]
    lane = lax.broadcasted_iota(jnp.int32, q.shape, 1)
    zero = jnp.zeros_like(q)
    qz = jnp.concatenate([jnp.where(lane < A_HEAD_DIM, q, zero),
                          jnp.where(lane >= A_HEAD_DIM, q, zero)], axis=0)
    m = jnp.full((1, 2 * tq), NEG_BIG, F32)
    acc = jnp.zeros((V_ROWS, 2 * tq), F32)
    tail_row = lax.broadcasted_iota(jnp.int32, (V_ROWS - A_VAL_DIM, tk), 0)
    tail = jnp.where(tail_row == 0, 1.0, 0.0).astype(BF16)

    def scores(c0):
        return lax.dot_general(k_ref[c0:c0 + tk, :], qz, (((1,), (1,)), ((), ())),
                               preferred_element_type=F32)

    st_next = scores(0)
    for c0 in range(0, nkeys, tk):
        st = st_next
        if c0 + tk < nkeys:
            st_next = scores(c0 + tk)
        m_new = jnp.maximum(m, jnp.max(st, axis=0, keepdims=True))
        alpha = jnp.exp2(m - m_new)
        p = jnp.exp2(st - m_new).astype(BF16)
        v_ext = jnp.concatenate([vt_ref[:, c0:c0 + tk], tail], axis=0)
        acc = alpha * acc + jnp.dot(v_ext, p, preferred_element_type=F32)
        m = m_new
    o1 = acc[0:A_VAL_DIM, 0:tq] / acc[A_VAL_DIM:A_VAL_DIM + 1, 0:tq]
    o2 = acc[0:A_VAL_DIM, tq:] / acc[A_VAL_DIM:A_VAL_DIM + 1, tq:]
    lp = lam_ref[...]
    lam = (jnp.exp(jnp.sum(lp[0:1] * lp[1:2], axis=1, keepdims=True))
           - jnp.exp(jnp.sum(lp[2:3] * lp[3:4], axis=1, keepdims=True)) + lam_init)
    o = (o1 - lam * o2).T
    ms = jnp.mean(o * o, axis=-1, keepdims=True)
    y = (o * lax.rsqrt(ms + SUBLN_EPS) * sw_ref[...]) * (1.0 - lam_init)
    o_ref[...] = (y * _silu(z_ref[...].astype(F32))).astype(o_ref.dtype)


def _attention(qk, zz, vt, lam_p, subln_w, lam_init, *, q_row0, n_q, k_row0, n_keys, tq, tk):
    qb0 = q_row0 // tq
    kb0 = k_row0 // n_keys
    return pl.pallas_call(
        functools.partial(_attn_kernel, tk=tk, lam_init=lam_init),
        grid=(A_HEADS, n_q // tq),
        in_specs=[pl.BlockSpec((tq, LANES), lambda h, i: (qb0 + i, h)),
                  pl.BlockSpec((n_keys, LANES), lambda h, i: (kb0, A_HEADS + h)),
                  pl.BlockSpec((A_VAL_DIM, n_keys), lambda h, i: (h, kb0)),
                  pl.BlockSpec((tq, LANES), lambda h, i: (qb0 + i, h)),
                  pl.BlockSpec(lam_p.shape, lambda h, i: (0, 0)),
                  pl.BlockSpec((1, A_VAL_DIM), lambda h, i: (0, 0))],
        out_specs=pl.BlockSpec((tq, LANES), lambda h, i: (i, h)),
        out_shape=jax.ShapeDtypeStruct((n_q, V_A), BF16),
        compiler_params=_params("parallel", "parallel"),
    )(qk, qk, vt, zz, lam_p, subln_w)


def _conv_kernel(x_ref, w_ref, o_ref, pad_ref, *, n_lat, n_ctx, rows):
    j = pl.program_id(0)
    half = (CONV_K - 1) // 2
    lat0 = 8
    ctx0 = lat0 + n_lat + 16
    zeros8 = jnp.zeros((8, LANES), F32)
    pad_ref[0:8, :] = zeros8
    pad_ref[lat0 + n_lat:lat0 + n_lat + 8, :] = zeros8
    pad_ref[lat0 + n_lat + 8:ctx0, :] = zeros8
    pad_ref[ctx0 + n_ctx:ctx0 + n_ctx + 8, :] = zeros8
    pad_ref[lat0:lat0 + n_lat, :] = x_ref[0:n_lat, :].astype(F32)
    pad_ref[ctx0:ctx0 + n_ctx, :] = x_ref[n_lat:n_lat + n_ctx, :].astype(F32)
    w = w_ref[...]
    is_qk = j < 2 * B_HEADS
    post = jnp.where(j < B_HEADS, B_DK ** -0.5, 1.0)
    for src0, dst0, n in ((lat0, 0, n_lat), (ctx0, n_lat, n_ctx)):
        for r in range(0, n, rows):
            acc = jnp.zeros((rows, LANES), F32)
            for tap in range(CONV_K):
                a = src0 + r + tap - half
                acc = acc + pad_ref[a:a + rows, :] * w[tap:tap + 1, :]
            y = _silu(acc)
            nrm = lax.rsqrt(jnp.sum(y * y, axis=-1, keepdims=True) + L2_EPS) * post
            y = y * jnp.where(is_qk, nrm, 1.0)
            o_ref[dst0 + r:dst0 + r + rows, :] = y.astype(o_ref.dtype)


def _conv_prep(pb, col0, conv_w, n_lat, n_ctx):
    t = pb.shape[0]
    nblk = CONV_CH // LANES
    blk0 = col0 // LANES
    return pl.pallas_call(
        functools.partial(_conv_kernel, n_lat=n_lat, n_ctx=n_ctx, rows=256),
        grid=(nblk,),
        in_specs=[pl.BlockSpec((t, LANES), lambda j: (0, blk0 + j)),
                  pl.BlockSpec((CONV_K, LANES), lambda j: (0, j))],
        out_specs=pl.BlockSpec((t, LANES), lambda j: (0, j)),
        out_shape=jax.ShapeDtypeStruct((t, CONV_CH), BF16),
        scratch_shapes=[pltpu.VMEM((t + 40, LANES), F32)],
        compiler_params=_params("parallel"),
    )(pb, conv_w)


def _gates_kernel(ab_ref, alog_ref, dtb_ref, o_ref):
    c = DELTA_CHUNK
    nh = B_HEADS
    r = lax.broadcasted_iota(jnp.int32, (c, c), 0)
    l = lax.broadcasted_iota(jnp.int32, (c, c), 1)
    upper = (r <= l).astype(F32)
    lower = (r >= l).astype(F32)
    neg_a = -jnp.exp(alog_ref[...])
    dtb = dtb_ref[...]
    for t0 in range(0, ab_ref.shape[1], c):
        x = ab_ref[0:2 * nh, t0:t0 + c] + dtb
        sp = jnp.maximum(x, 0.0) + jnp.log1p(jnp.exp(-jnp.abs(x)))
        g = neg_a * sp
        o_ref[0:nh, t0:t0 + c] = jnp.dot(g[0:nh], upper, preferred_element_type=F32,
                                          precision=lax.Precision.HIGHEST)
        o_ref[nh:2 * nh, t0:t0 + c] = jnp.dot(g[nh:2 * nh], lower, preferred_element_type=F32,
                                               precision=lax.Precision.HIGHEST)
        o_ref[2 * nh:4 * nh, t0:t0 + c] = jax.nn.sigmoid(ab_ref[2 * nh:4 * nh, t0:t0 + c])


def _gates(ab_t, alog_b, dtb_b):
    rows, t = ab_t.shape
    tl = _tile(t, 1408, 256)
    return pl.pallas_call(
        _gates_kernel,
        grid=(t // tl,),
        in_specs=[pl.BlockSpec((rows, tl), lambda i: (0, i)),
                  pl.BlockSpec(alog_b.shape, lambda i: (0, 0)),
                  pl.BlockSpec(dtb_b.shape, lambda i: (0, 0))],
        out_specs=pl.BlockSpec((rows, tl), lambda i: (0, i)),
        out_shape=jax.ShapeDtypeStruct((rows, t), F32),
        compiler_params=_params("parallel"),
    )(ab_t, alog_b, dtb_b)


def _delta_prep_kernel(q_ref, k_ref, v_ref, g_ref, aq_ref, bo_ref, e_ref, *, n_chunks):
    c = DELTA_CHUNK
    row = lax.broadcasted_iota(jnp.int32, (c, c), 0)
    col = lax.broadcasted_iota(jnp.int32, (c, c), 1)
    eye = row == col
    eye_f = eye.astype(F32)
    n_levels = int(math.log2(c))
    same_blk = [(row >> s) == (col >> s) for s in range(n_levels + 1)]
    off_blk = [None] + [jnp.logical_and(same_blk[lv + 1], jnp.logical_not(same_blk[lv]))
                        for lv in range(1, n_levels)]
    nt = (((1,), (1,)), ((), ()))

    def to_col(r):
        return jnp.sum(jnp.where(eye, jnp.broadcast_to(r, (c, c)), 0.0), axis=1, keepdims=True)

    chains = []
    for ci in range(n_chunks):
        q = q_ref[ci * c:(ci + 1) * c, :]
        k = k_ref[ci * c:(ci + 1) * c, :]
        v = v_ref[ci * c:(ci + 1) * c, :]
        rows = g_ref[0, ci]
        kk = lax.dot_general(k, k, nt, preferred_element_type=F32)
        qk = lax.dot_general(q, k, nt, preferred_element_type=F32)
        for d in range(2):
            if d == 0:
                g_r, b_r = rows[0:1], rows[2:3]
                g_last = g_r[:, c - 1:c]
                mask, strict = row >= col, row > col
            else:
                g_r, b_r = rows[1:2], rows[3:4]
                g_last = g_r[:, 0:1]
                mask, strict = row <= col, row < col
            g_c = to_col(g_r)
            b_c = to_col(b_r)
            decay = jnp.where(mask, jnp.exp(jnp.where(mask, g_c - g_r, 0.0)), 0.0)
            n_mat = jnp.where(strict, kk * b_c * decay, 0.0)
            chains.append(dict(ci=ci, d=d, q=q, k=k, v=v, qk=qk, g_c=g_c, b_c=b_c, g_last=g_last, mask=mask,
                               decay=decay, n_mat=n_mat, x=eye_f - jnp.where(same_blk[1], n_mat, 0.0)))
    for lv in range(1, n_levels):
        for ch in chains:
            ch["xb"] = ch["x"].astype(BF16)
            ch["a"] = jnp.dot(ch["xb"], jnp.where(off_blk[lv], ch["n_mat"], 0.0).astype(BF16),
                              preferred_element_type=F32)
        for ch in chains:
            ch["x"] = ch["x"] - jnp.dot(ch["a"].astype(BF16), ch["xb"], preferred_element_type=F32)
    for ch in chains:
        kf = ch["k"].astype(F32)
        ch["e_g"] = jnp.exp(ch["g_c"])
        kb = kf * ch["b_c"]
        rhs = jnp.concatenate([(kb * ch["e_g"]).astype(BF16),
                               (ch["v"].astype(F32) * ch["b_c"]).astype(BF16)], axis=1)
        ch["wu"] = jnp.dot(ch["x"].astype(BF16), rhs, preferred_element_type=F32).astype(BF16)
        ch["attn"] = jnp.where(ch["mask"], ch["qk"] * ch["decay"], 0.0).astype(BF16)
        ch["ke_t"] = (kf * jnp.exp(ch["g_last"] - ch["g_c"])).T.astype(BF16)
    for ch in chains:
        ch["m1"] = jnp.dot(ch["ke_t"], ch["wu"], preferred_element_type=F32)
        ch["m2"] = jnp.dot(ch["attn"], ch["wu"], preferred_element_type=F32)
    for ch in chains:
        ci, d, m1, m2 = ch["ci"], ch["d"], ch["m1"], ch["m2"]
        aq_ref[d, 0, ci, 0:c, :] = (-m1[:, 0:B_DV]).astype(BF16)
        aq_ref[d, 0, ci, c:2 * c, :] = (ch["q"].astype(F32) * ch["e_g"] - m2[:, 0:B_DV]).astype(BF16)
        bo_ref[d, 0, ci, 0:c, :] = m1[:, B_DV:].astype(BF16)
        bo_ref[d, 0, ci, c:2 * c, :] = m2[:, B_DV:].astype(BF16)
    for ci in range(n_chunks):
        e_f, e_b = (jnp.broadcast_to(jnp.exp(chains[2 * ci + d]["g_last"]), (1, c)) for d in range(2))
        e_ref[0, ci] = jnp.concatenate([e_f, e_b, jnp.zeros((6, c), F32)], axis=0)


def _delta_prep(qkv, g4):
    t = qkv.shape[0]
    c = DELTA_CHUNK
    nch = t // c
    g = _tile(nch, 11, 6)
    rows = g * c
    return pl.pallas_call(
        functools.partial(_delta_prep_kernel, n_chunks=g),
        grid=(B_HEADS, nch // g),
        in_specs=[pl.BlockSpec((rows, LANES), lambda h, i: (i, h)),
                  pl.BlockSpec((rows, LANES), lambda h, i: (i, B_HEADS + h)),
                  pl.BlockSpec((rows, LANES), lambda h, i: (i, 2 * B_HEADS + h)),
                  pl.BlockSpec((1, g, 8, c), lambda h, i: (h, i, 0, 0))],
        out_specs=[pl.BlockSpec((2, 1, g, 2 * c, B_DV), lambda h, i: (0, h, i, 0, 0)),
                   pl.BlockSpec((2, 1, g, 2 * c, B_DV), lambda h, i: (0, h, i, 0, 0)),
                   pl.BlockSpec((1, g, 8, c), lambda h, i: (h, i, 0, 0))],
        out_shape=[jax.ShapeDtypeStruct((2, B_HEADS, nch, 2 * c, B_DV), BF16),
                   jax.ShapeDtypeStruct((2, B_HEADS, nch, 2 * c, B_DV), BF16),
                   jax.ShapeDtypeStruct((B_HEADS, nch, 8, c), F32)],
        compiler_params=_params("parallel", "parallel"),
    )(qkv, qkv, qkv, g4)


def _delta_scan_kernel(aqf_ref, bof_ref, ef_ref, aqb_ref, bob_ref, eb_ref, of_ref, ob_ref, st_ref):
    c = DELTA_CHUNK

    @pl.when(pl.program_id(0) == 0)
    def _():
        st_ref[...] = jnp.zeros_like(st_ref)

    for d, (aq_ref, bo_ref, e_ref, o_ref) in enumerate(((aqf_ref, bof_ref, ef_ref, of_ref),
                                                        (aqb_ref, bob_ref, eb_ref, ob_ref))):
        for h in range(B_HEADS):
            state = st_ref[d, h]
            res = jnp.dot(aq_ref[0, h, 0], state.astype(BF16), preferred_element_type=F32)
            bo = bo_ref[0, h, 0].astype(F32)
            o_ref[:, h * B_DV:(h + 1) * B_DV] = res[c:] + bo[c:]
            st_ref[d, h] = state * e_ref[h, 0, d:d + 1, 0:1] + res[0:c] + bo[0:c]


def _delta_scan(aq, bo, ee, n_lat, n_ctx):
    c = DELTA_CHUNK
    nch = aq.shape[2]
    n_lat_c = n_lat // c
    n_ctx_c = n_ctx // c
    t = nch * c

    def fwd(s):
        return jnp.where(s < n_ctx_c, s + n_lat_c, s - n_ctx_c)

    def bwd(s):
        return nch - 1 - s

    def big(direction, order):
        return pl.BlockSpec((1, B_HEADS, 1, 2 * c, B_DV), lambda s: (direction, 0, order(s), 0, 0))

    def small(order):
        return pl.BlockSpec((B_HEADS, 1, 8, c), lambda s: (0, order(s), 0, 0))

    return pl.pallas_call(
        _delta_scan_kernel,
        grid=(nch,),
        in_specs=[big(0, fwd), big(0, fwd), small(fwd), big(1, bwd), big(1, bwd), small(bwd)],
        out_specs=[pl.BlockSpec((c, V_B), lambda s: (fwd(s), 0)),
                   pl.BlockSpec((c, V_B), lambda s: (bwd(s), 0))],
        out_shape=[jax.ShapeDtypeStruct((t, V_B), F32), jax.ShapeDtypeStruct((t, V_B), F32)],
        scratch_shapes=[pltpu.VMEM((2, B_HEADS, B_DK, B_DV), F32)],
        compiler_params=_params("arbitrary"),
    )(aq, bo, ee, aq, bo, ee)


def _delta_gated(of_ref, ob_ref, z_ref, nw_ref):
    nw = nw_ref[...]
    heads = []
    for h in range(B_HEADS):
        sl = slice(h * B_DV, (h + 1) * B_DV)
        o = of_ref[:, sl] + ob_ref[:, sl]
        ms = jnp.mean(o * o, axis=-1, keepdims=True)
        y = (o * lax.rsqrt(ms + NORM_EPS) * nw) * _silu(z_ref[:, sl].astype(F32))
        heads.append(y.astype(BF16))
    return jnp.concatenate(heads, axis=1)


def _out_kernel(ya_ref, yac_ref, of_ref, ob_ref, z_ref, dnw_ref, wa_ref, wb_ref, x_ref, mod_ref, nw_ref, modn_ref,
                o_ref, xn_ref, *, n_lat_tiles):
    is_ctx = pl.program_id(0) >= n_lat_tiles
    d = x_ref.shape[1]
    gate = jnp.where(is_ctx, mod_ref[1:2, :], mod_ref[0:1, :])
    ya = jnp.where(is_ctx, yac_ref[...], ya_ref[...])
    acc = jnp.dot(ya, wa_ref[0], preferred_element_type=F32)
    acc = acc + jnp.dot(_delta_gated(of_ref, ob_ref, z_ref, dnw_ref), wb_ref[0], preferred_element_type=F32)
    x = x_ref[...] + gate * acc
    o_ref[...] = x
    shift = jnp.where(is_ctx, modn_ref[1:2, 0:d], modn_ref[0:1, 0:d])
    scale = jnp.where(is_ctx, modn_ref[1:2, d:2 * d], modn_ref[0:1, d:2 * d])
    xn_ref[...] = _modulated_norm(x, nw_ref[...], shift, scale).astype(xn_ref.dtype)


def _out_final_kernel(ya_ref, of_ref, ob_ref, z_ref, dnw_ref, wa_ref, wb_ref, x_ref, mod_ref, fw_ref, o_ref):
    acc = jnp.dot(ya_ref[...], wa_ref[0], preferred_element_type=F32)
    acc = acc + jnp.dot(_delta_gated(of_ref, ob_ref, z_ref, dnw_ref), wb_ref[0], preferred_element_type=F32)
    x = x_ref[...] + mod_ref[0:1, :] * acc
    ms = jnp.mean(x * x, axis=-1, keepdims=True)
    o_ref[...] = x * lax.rsqrt(ms + NORM_EPS) * fw_ref[...]


def _delta_out_specs(tm):
    return [pl.BlockSpec((tm, V_B), lambda i: (i, 0)),
            pl.BlockSpec((tm, V_B), lambda i: (i, 0)),
            pl.BlockSpec((tm, V_B), lambda i: (i, V_A // V_B)),
            pl.BlockSpec((1, B_DV), lambda i: (0, 0))]


def _out_proj(ya, ya_ctx, out_f, out_b, zz, dn_norm_w, w_out, layer, xcat, mod, nw_next, mod_next, n_lat):
    t, d = xcat.shape
    tm = 256
    n_lat_tiles = n_lat // tm
    return pl.pallas_call(
        functools.partial(_out_kernel, n_lat_tiles=n_lat_tiles),
        grid=(t // tm,),
        in_specs=[pl.BlockSpec((tm, V_A), lambda i: (jnp.minimum(i, n_lat_tiles - 1), 0)),
                  pl.BlockSpec((tm, V_A), lambda i: (jnp.maximum(i - n_lat_tiles, 0), 0)),
                  *_delta_out_specs(tm),
                  pl.BlockSpec((1, V_A, d), lambda i: (layer, 0, 0)),
                  pl.BlockSpec((1, V_B, d), lambda i: (layer, V_A // V_B, 0)),
                  pl.BlockSpec((tm, d), lambda i: (i, 0)),
                  pl.BlockSpec((8, d), lambda i: (0, 2)),
                  pl.BlockSpec((1, d), lambda i: (0, 0)),
                  pl.BlockSpec((8, 2 * d), lambda i: (0, 0))],
        out_specs=[pl.BlockSpec((tm, d), lambda i: (i, 0)),
                   pl.BlockSpec((tm, d), lambda i: (i, 0))],
        out_shape=[jax.ShapeDtypeStruct((t, d), F32), jax.ShapeDtypeStruct((t, d), BF16)],
        compiler_params=_params("parallel"),
    )(ya, ya_ctx, out_f, out_b, zz, dn_norm_w, w_out, w_out, xcat, mod, nw_next, mod_next)


def _out_proj_final(ya, out_f, out_b, zz, dn_norm_w, w_out, layer, xcat, mod, final_w, n_lat):
    d = xcat.shape[1]
    tm = 256
    return pl.pallas_call(
        _out_final_kernel,
        grid=(n_lat // tm,),
        in_specs=[pl.BlockSpec((tm, V_A), lambda i: (i, 0)),
                  *_delta_out_specs(tm),
                  pl.BlockSpec((1, V_A, d), lambda i: (layer, 0, 0)),
                  pl.BlockSpec((1, V_B, d), lambda i: (layer, V_A // V_B, 0)),
                  pl.BlockSpec((tm, d), lambda i: (i, 0)),
                  pl.BlockSpec((8, d), lambda i: (0, 2)),
                  pl.BlockSpec((1, d), lambda i: (0, 0))],
        out_specs=pl.BlockSpec((tm, d), lambda i: (i, 0)),
        out_shape=jax.ShapeDtypeStruct((n_lat, d), F32),
        compiler_params=_params("parallel"),
    )(ya, out_f, out_b, zz, dn_norm_w, w_out, w_out, xcat, mod, final_w)


def _rope_tables(n_lat, n_ctx):
    n_rows = n_lat // GRID_W
    rows = jnp.repeat(jnp.arange(n_rows, dtype=F32), GRID_W)
    cols = jnp.tile(jnp.arange(GRID_W, dtype=F32), n_rows)
    n_freq = A_HEAD_DIM // 4
    inv = ROPE_THETA ** (-jnp.arange(n_freq, dtype=F32) / n_freq)
    ang = jnp.concatenate([rows[:, None] * inv, cols[:, None] * inv], axis=-1)
    cos = jnp.cos(ang)
    sin = jnp.sin(ang)
    cos_t = jnp.tile(cos, (1, LANES // (A_HEAD_DIM // 2)))
    sin_t = jnp.tile(jnp.concatenate([-sin, sin], axis=-1), (1, LANES // A_HEAD_DIM))
    cos_t = jnp.concatenate([cos_t, jnp.ones((n_ctx, LANES), F32)], axis=0)
    sin_t = jnp.concatenate([sin_t, jnp.zeros((n_ctx, LANES), F32)], axis=0)
    return cos_t, sin_t


def kernel(x, c, ctx, c_ctx, w_ada, b_ada, norm_w, w_in, conv_w, dn_a_log, dn_dt_bias, dn_norm_w,
           diff_lambda, subln_w, w_out, final_norm_w):
    depth = w_ada.shape[0]
    n_lat = x.shape[1]
    n_ctx = ctx.shape[1]
    t = n_lat + n_ctx
    d = x.shape[2]
    assert x.shape[0] == 1 and d == D_MODEL
    assert n_lat % 512 == 0 and n_ctx == 256 and t % 768 == 0

    xcat = jnp.concatenate([x[0], ctx[0]], axis=0)
    cc = jnp.zeros((8, d), F32).at[0].set(c[0]).at[1].set(c_ctx)
    mods = _ada_all(cc, w_ada, b_ada)
    cos_t, sin_t = _rope_tables(n_lat, n_ctx)

    o_q, o_v, o_z, o_b, o_zb, o_ab = 0, 2 * QK_A, 2 * QK_A + V_A, 2 * QK_A + 2 * V_A, \
        2 * QK_A + 2 * V_A + CONV_CH, 2 * QK_A + 2 * V_A + CONV_CH + V_B
    w_in_bf = w_in[:, :, :o_ab].astype(BF16)
    w_out_bf = w_out.astype(BF16)
    w_ab = jnp.pad(w_in[:, :, o_ab:], ((0, 0), (0, 0), (0, LANES - 4 * B_HEADS))).astype(BF16)

    nch = t // DELTA_CHUNK
    xn = _norm_mod(xcat, norm_w[0][None, :], mods[0], n_lat)
    for i in range(depth):
        last = i == depth - 1
        lam_init = 0.8 - 0.6 * math.exp(-0.3 * i)
        mod = mods[i]
        qk = _proj(xn, w_in_bf, i, [(o_q, 2 * QK_A)], BF16, 256, rope_q_cols=QK_A, rope=(cos_t, sin_t))
        vt = _proj(xn, w_in_bf, i, [(o_v, V_A)], BF16, 512, transposed=True)
        zz = _proj(xn, w_in_bf, i, [(o_z, V_A), (o_zb, V_B), (o_b, CONV_CH)], BF16, 512)
        ab = _proj(xn, w_ab, i, [(0, LANES)], F32, LANES)

        lam_p = diff_lambda[i]
        sw = subln_w[i][None, :]
        ya = _attention(qk, zz, vt, lam_p, sw, lam_init, q_row0=0, n_q=n_lat, k_row0=0, n_keys=t,
                        tq=_tile(n_lat, 1024, 512), tk=768)
        if not last:
            ya_ctx = _attention(qk, zz, vt, lam_p, sw, lam_init, q_row0=n_lat, n_q=n_ctx, k_row0=n_lat,
                                n_keys=n_ctx, tq=256, tk=256)

        qkv = _conv_prep(zz, V_A + V_B, conv_w[i], n_lat, n_ctx)
        ab_t = ab[:, 0:4 * B_HEADS].T
        alog_b = jnp.broadcast_to(dn_a_log[i].reshape(2 * B_HEADS, 1), (2 * B_HEADS, DELTA_CHUNK))
        dtb_b = jnp.broadcast_to(dn_dt_bias[i].reshape(2 * B_HEADS, 1), (2 * B_HEADS, DELTA_CHUNK))
        gates = _gates(ab_t, alog_b, dtb_b)
        g4 = gates.reshape(4, B_HEADS, nch, DELTA_CHUNK).transpose(1, 2, 0, 3)
        g4 = jnp.concatenate([g4, jnp.zeros_like(g4)], axis=2)
        aq, bo, ee = _delta_prep(qkv, g4)
        out_f, out_b = _delta_scan(aq, bo, ee, n_lat, n_ctx)
        dnw = dn_norm_w[i][None, :]

        if not last:
            xcat, xn = _out_proj(ya, ya_ctx, out_f, out_b, zz, dnw, w_out_bf, i, xcat, mod,
                                 norm_w[i + 1][None, :], mods[i + 1], n_lat)
        else:
            out = _out_proj_final(ya, out_f, out_b, zz, dnw, w_out_bf, i, xcat, mod, final_norm_w[None, :], n_lat)
    return out[None]
```

```python
import functools
import math

import jax
import jax.numpy as jnp
from jax import lax
from jax.experimental import pallas as pl
from jax.experimental.pallas import tpu as pltpu

F32 = jnp.float32
BF16 = jnp.bfloat16

LANES = 128
VMEM_LIMIT_BYTES = 56 * 1024 * 1024

D_MODEL = 2048
GRID_W = 64
A_HEADS = 8
A_HEAD_DIM = 64
A_VAL_DIM = 2 * A_HEAD_DIM
QK_A = A_HEADS * 2 * A_HEAD_DIM
V_A = A_HEADS * A_VAL_DIM
B_HEADS = 8
B_DK = 128
B_DV = 128
QK_B = B_HEADS * B_DK
V_B = B_HEADS * B_DV
CONV_K = 5
CONV_CH = 2 * QK_B + V_B
ROPE_THETA = 10000.0
NORM_EPS = 1e-6
SUBLN_EPS = 1e-5
L2_EPS = 1e-6

DELTA_CHUNK = 128
V_ROWS = A_VAL_DIM + 16
NEG_BIG = -1e30
SCORE_LOOKAHEAD = 2
Q_SCALE = A_HEAD_DIM ** -0.5 * math.log2(math.e)


def _params(*sem, flags=None):
    return pltpu.CompilerParams(dimension_semantics=sem, vmem_limit_bytes=VMEM_LIMIT_BYTES, flags=flags)


def _silu(x):
    return x * jax.nn.sigmoid(x)


def _tile(n, preferred, fallback):
    return preferred if n % preferred == 0 else fallback


def _ada_kernel(c_ref, w_ref, b_ref, o_ref):
    s = _silu(c_ref[...])
    o_ref[0] = jnp.dot(s.astype(BF16), w_ref[0].astype(BF16), preferred_element_type=F32) + b_ref[0]


def _ada_all(cc, w_ada, b_ada):
    depth, d, n = w_ada.shape
    tn = 512
    return pl.pallas_call(
        _ada_kernel,
        grid=(depth, n // tn),
        in_specs=[pl.BlockSpec((8, d), lambda l, j: (0, 0)),
                  pl.BlockSpec((1, d, tn), lambda l, j: (l, 0, j)),
                  pl.BlockSpec((1, 1, tn), lambda l, j: (l, 0, j))],
        out_specs=pl.BlockSpec((1, 8, tn), lambda l, j: (l, 0, j)),
        out_shape=jax.ShapeDtypeStruct((depth, 8, n), F32),
        compiler_params=_params("parallel", "parallel"),
    )(cc, w_ada, b_ada.reshape(depth, 1, n))


def _modulated_norm(x, nw, shift, scale):
    ms = jnp.mean(x * x, axis=-1, keepdims=True)
    return (x * lax.rsqrt(ms + NORM_EPS) * nw) * (1.0 + scale) + shift


def _norm_kernel(x_ref, nw_ref, mod_ref, o_ref, *, n_lat_tiles):
    is_ctx = pl.program_id(0) >= n_lat_tiles
    d = x_ref.shape[1]
    shift = jnp.where(is_ctx, mod_ref[1:2, 0:d], mod_ref[0:1, 0:d])
    scale = jnp.where(is_ctx, mod_ref[1:2, d:2 * d], mod_ref[0:1, d:2 * d])
    o_ref[...] = _modulated_norm(x_ref[...], nw_ref[...], shift, scale).astype(o_ref.dtype)


def _norm_mod(xcat, nw, mod, n_lat):
    t, d = xcat.shape
    tm = 256
    return pl.pallas_call(
        functools.partial(_norm_kernel, n_lat_tiles=n_lat // tm),
        grid=(t // tm,),
        in_specs=[pl.BlockSpec((tm, d), lambda i: (i, 0)),
                  pl.BlockSpec((1, d), lambda i: (0, 0)),
                  pl.BlockSpec((8, 2 * d), lambda i: (0, 0))],
        out_specs=pl.BlockSpec((tm, d), lambda i: (i, 0)),
        out_shape=jax.ShapeDtypeStruct((t, d), BF16),
        compiler_params=_params("parallel"),
    )(xcat, nw, mod)


def _mm_kernel(x_ref, w_ref, o_ref):
    o_ref[...] = jnp.dot(x_ref[...], w_ref[0], preferred_element_type=F32).astype(o_ref.dtype)


def _mm_t_kernel(x_ref, w_ref, o_ref, acc_ref):
    acc_ref[...] = jnp.dot(x_ref[...], w_ref[0], preferred_element_type=F32)
    o_ref[...] = acc_ref[...].T.astype(o_ref.dtype)


def _mm_rope_kernel(x_ref, w_ref, cos_ref, sin_ref, o_ref, *, q_tiles):
    scale = jnp.where(pl.program_id(1) < q_tiles, Q_SCALE, 1.0)
    t = jnp.dot(x_ref[...], w_ref[0], preferred_element_type=F32) * scale
    c = cos_ref[...]
    s = sin_ref[...]
    lane = lax.broadcasted_iota(jnp.int32, c.shape, 1)
    first = (lane % A_HEAD_DIM) < (A_HEAD_DIM // 2)
    for a in range(0, t.shape[1], LANES):
        th = t[:, a:a + LANES]
        sw = jnp.where(first, pltpu.roll(th, LANES - A_HEAD_DIM // 2, 1), pltpu.roll(th, A_HEAD_DIM // 2, 1))
        o_ref[:, a:a + LANES] = (th * c + sw * s).astype(o_ref.dtype)


def _proj(xn, w, layer, col_segments, out_dtype, tn, rope_q_cols=None, rope=None, transposed=False):
    t, d = xn.shape
    tm = _tile(t, 1408, 256)
    assert all(c0 % tn == 0 and width % tn == 0 for c0, width in col_segments)
    n_tiles = sum(width // tn for _, width in col_segments)
    scratch = []
    bounds, offs, acc = [], [], 0
    for c0, width in col_segments:
        offs.append(c0 // tn - acc)
        acc += width // tn
        bounds.append(acc)

    def wcol(j):
        col = j + offs[-1]
        for bnd, off in zip(reversed(bounds[:-1]), reversed(offs[:-1])):
            col = jnp.where(j < bnd, j + off, col)
        return col

    in_specs = [pl.BlockSpec((tm, d), lambda i, j: (i, 0)),
                pl.BlockSpec((1, d, tn), lambda i, j: (layer, 0, wcol(j)))]
    args = [xn, w]
    kern = _mm_kernel
    out_spec = pl.BlockSpec((tm, tn), lambda i, j: (i, j))
    out_shape = (t, n_tiles * tn)
    if rope is not None:
        in_specs += [pl.BlockSpec((tm, LANES), lambda i, j: (i, 0))] * 2
        args += list(rope)
        kern = functools.partial(_mm_rope_kernel, q_tiles=rope_q_cols // tn)
    if transposed:
        kern = _mm_t_kernel
        out_spec = pl.BlockSpec((tn, tm), lambda i, j: (j, i))
        out_shape = (n_tiles * tn, t)
        scratch = [pltpu.VMEM((tm, tn), F32)]
    return pl.pallas_call(
        kern,
        grid=(t // tm, n_tiles),
        in_specs=in_specs,
        out_specs=out_spec,
        out_shape=jax.ShapeDtypeStruct(out_shape, out_dtype),
        scratch_shapes=scratch,
        compiler_params=_params("parallel", "parallel"),
    )(*args)


def _attn_kernel(q_ref, k_ref, vt_ref, z_ref, lam_ref, sw_ref, o_ref, *, tk, lam_init):
    tq = q_ref.shape[0]
    nkeys = k_ref.shape[0]
    q = q_ref[...]
    lane = lax.broadcasted_iota(jnp.int32, q.shape, 1)
    zero = jnp.zeros_like(q)
    qz = jnp.concatenate([jnp.where(lane < A_HEAD_DIM, q, zero),
                          jnp.where(lane >= A_HEAD_DIM, q, zero)], axis=0)
    m = jnp.full((1, 2 * tq), NEG_BIG, F32)
    acc = jnp.zeros((V_ROWS, 2 * tq), F32)
    tail_row = lax.broadcasted_iota(jnp.int32, (V_ROWS - A_VAL_DIM, tk), 0)
    tail = jnp.where(tail_row == 0, 1.0, 0.0).astype(BF16)

    def scores(c0):
        return lax.dot_general(k_ref[c0:c0 + tk, :], qz, (((1,), (1,)), ((), ())),
                               preferred_element_type=F32)

    starts = list(range(0, nkeys, tk))
    queue = [scores(c0) for c0 in starts[:SCORE_LOOKAHEAD]]
    for idx, c0 in enumerate(starts):
        st = queue.pop(0)
        if idx + SCORE_LOOKAHEAD < len(starts):
            queue.append(scores(starts[idx + SCORE_LOOKAHEAD]))
        m_new = jnp.maximum(m, jnp.max(st, axis=0, keepdims=True))
        alpha = jnp.exp2(m - m_new)
        p = jnp.exp2(st - m_new).astype(BF16)
        v_ext = jnp.concatenate([vt_ref[:, c0:c0 + tk], tail], axis=0)
        acc = alpha * acc + jnp.dot(v_ext, p, preferred_element_type=F32)
        m = m_new
    o1 = acc[0:A_VAL_DIM, 0:tq] / acc[A_VAL_DIM:A_VAL_DIM + 1, 0:tq]
    o2 = acc[0:A_VAL_DIM, tq:] / acc[A_VAL_DIM:A_VAL_DIM + 1, tq:]
    lp = lam_ref[...]
    lam = (jnp.exp(jnp.sum(lp[0:1] * lp[1:2], axis=1, keepdims=True))
           - jnp.exp(jnp.sum(lp[2:3] * lp[3:4], axis=1, keepdims=True)) + lam_init)
    o = (o1 - lam * o2).T
    ms = jnp.mean(o * o, axis=-1, keepdims=True)
    y = (o * lax.rsqrt(ms + SUBLN_EPS) * sw_ref[...]) * (1.0 - lam_init)
    o_ref[...] = (y * _silu(z_ref[...].astype(F32))).astype(o_ref.dtype)


def _attention(qk, zz, vt, lam_p, subln_w, lam_init, *, q_row0, n_q, k_row0, n_keys, tq, tk):
    qb0 = q_row0 // tq
    kb0 = k_row0 // n_keys
    return pl.pallas_call(
        functools.partial(_attn_kernel, tk=tk, lam_init=lam_init),
        grid=(A_HEADS, n_q // tq),
        in_specs=[pl.BlockSpec((tq, LANES), lambda h, i: (qb0 + i, h)),
                  pl.BlockSpec((n_keys, LANES), lambda h, i: (kb0, A_HEADS + h)),
                  pl.BlockSpec((A_VAL_DIM, n_keys), lambda h, i: (h, kb0)),
                  pl.BlockSpec((tq, LANES), lambda h, i: (qb0 + i, h)),
                  pl.BlockSpec(lam_p.shape, lambda h, i: (0, 0)),
                  pl.BlockSpec((1, A_VAL_DIM), lambda h, i: (0, 0))],
        out_specs=pl.BlockSpec((tq, LANES), lambda h, i: (i, h)),
        out_shape=jax.ShapeDtypeStruct((n_q, V_A), BF16),
        compiler_params=_params("parallel", "parallel"),
    )(qk, qk, vt, zz, lam_p, subln_w)


def _conv_kernel(x_ref, w_ref, o_ref, pad_ref, *, n_lat, n_ctx, rows):
    j = pl.program_id(0)
    half = (CONV_K - 1) // 2
    lat0 = 8
    ctx0 = lat0 + n_lat + 16
    zeros8 = jnp.zeros((8, LANES), F32)
    pad_ref[0:8, :] = zeros8
    pad_ref[lat0 + n_lat:lat0 + n_lat + 8, :] = zeros8
    pad_ref[lat0 + n_lat + 8:ctx0, :] = zeros8
    pad_ref[ctx0 + n_ctx:ctx0 + n_ctx + 8, :] = zeros8
    pad_ref[lat0:lat0 + n_lat, :] = x_ref[0:n_lat, :].astype(F32)
    pad_ref[ctx0:ctx0 + n_ctx, :] = x_ref[n_lat:n_lat + n_ctx, :].astype(F32)
    w = w_ref[...]
    is_qk = j < 2 * B_HEADS
    post = jnp.where(j < B_HEADS, B_DK ** -0.5, 1.0)
    for src0, dst0, n in ((lat0, 0, n_lat), (ctx0, n_lat, n_ctx)):
        for r in range(0, n, rows):
            acc = jnp.zeros((rows, LANES), F32)
            for tap in range(CONV_K):
                a = src0 + r + tap - half
                acc = acc + pad_ref[a:a + rows, :] * w[tap:tap + 1, :]
            y = _silu(acc)
            nrm = lax.rsqrt(jnp.sum(y * y, axis=-1, keepdims=True) + L2_EPS) * post
            y = y * jnp.where(is_qk, nrm, 1.0)
            o_ref[dst0 + r:dst0 + r + rows, :] = y.astype(o_ref.dtype)


def _conv_prep(pb, col0, conv_w, n_lat, n_ctx):
    t = pb.shape[0]
    nblk = CONV_CH // LANES
    blk0 = col0 // LANES
    return pl.pallas_call(
        functools.partial(_conv_kernel, n_lat=n_lat, n_ctx=n_ctx, rows=256),
        grid=(nblk,),
        in_specs=[pl.BlockSpec((t, LANES), lambda j: (0, blk0 + j)),
                  pl.BlockSpec((CONV_K, LANES), lambda j: (0, j))],
        out_specs=pl.BlockSpec((t, LANES), lambda j: (0, j)),
        out_shape=jax.ShapeDtypeStruct((t, CONV_CH), BF16),
        scratch_shapes=[pltpu.VMEM((t + 40, LANES), F32)],
        compiler_params=_params("parallel"),
    )(pb, conv_w)


def _gates_kernel(ab_ref, alog_ref, dtb_ref, o_ref):
    c = DELTA_CHUNK
    nh = B_HEADS
    r = lax.broadcasted_iota(jnp.int32, (c, c), 0)
    l = lax.broadcasted_iota(jnp.int32, (c, c), 1)
    upper = (r <= l).astype(F32)
    lower = (r >= l).astype(F32)
    neg_a = -jnp.exp(alog_ref[...])
    dtb = dtb_ref[...]
    for t0 in range(0, ab_ref.shape[1], c):
        x = ab_ref[0:2 * nh, t0:t0 + c] + dtb
        sp = jnp.maximum(x, 0.0) + jnp.log1p(jnp.exp(-jnp.abs(x)))
        g = neg_a * sp
        o_ref[0:nh, t0:t0 + c] = jnp.dot(g[0:nh], upper, preferred_element_type=F32,
                                          precision=lax.Precision.HIGHEST)
        o_ref[nh:2 * nh, t0:t0 + c] = jnp.dot(g[nh:2 * nh], lower, preferred_element_type=F32,
                                               precision=lax.Precision.HIGHEST)
        o_ref[2 * nh:4 * nh, t0:t0 + c] = jax.nn.sigmoid(ab_ref[2 * nh:4 * nh, t0:t0 + c])


def _gates(ab_t, alog_b, dtb_b):
    rows, t = ab_t.shape
    tl = _tile(t, 1408, 256)
    return pl.pallas_call(
        _gates_kernel,
        grid=(t // tl,),
        in_specs=[pl.BlockSpec((rows, tl), lambda i: (0, i)),
                  pl.BlockSpec(alog_b.shape, lambda i: (0, 0)),
                  pl.BlockSpec(dtb_b.shape, lambda i: (0, 0))],
        out_specs=pl.BlockSpec((rows, tl), lambda i: (0, i)),
        out_shape=jax.ShapeDtypeStruct((rows, t), F32),
        compiler_params=_params("parallel"),
    )(ab_t, alog_b, dtb_b)


def _delta_prep_kernel(q_ref, k_ref, v_ref, g_ref, aq_ref, bo_ref, e_ref, *, n_chunks):
    c = DELTA_CHUNK
    row = lax.broadcasted_iota(jnp.int32, (c, c), 0)
    col = lax.broadcasted_iota(jnp.int32, (c, c), 1)
    eye = row == col
    eye_f = eye.astype(F32)
    n_levels = int(math.log2(c))
    same_blk = [(row >> s) == (col >> s) for s in range(n_levels + 1)]
    off_blk = [None] + [jnp.logical_and(same_blk[lv + 1], jnp.logical_not(same_blk[lv]))
                        for lv in range(1, n_levels)]
    nt = (((1,), (1,)), ((), ()))

    def to_col(r):
        return jnp.sum(jnp.where(eye, jnp.broadcast_to(r, (c, c)), 0.0), axis=1, keepdims=True)

    chains = []
    for ci in range(n_chunks):
        q = q_ref[ci * c:(ci + 1) * c, :]
        k = k_ref[ci * c:(ci + 1) * c, :]
        v = v_ref[ci * c:(ci + 1) * c, :]
        rows = g_ref[0, ci]
        kk = lax.dot_general(k, k, nt, preferred_element_type=F32)
        qk = lax.dot_general(q, k, nt, preferred_element_type=F32)
        for d in range(2):
            if d == 0:
                g_r, b_r = rows[0:1], rows[2:3]
                g_last = g_r[:, c - 1:c]
                mask, strict = row >= col, row > col
            else:
                g_r, b_r = rows[1:2], rows[3:4]
                g_last = g_r[:, 0:1]
                mask, strict = row <= col, row < col
            g_c = to_col(g_r)
            b_c = to_col(b_r)
            decay = jnp.where(mask, jnp.exp(jnp.where(mask, g_c - g_r, 0.0)), 0.0)
            n_mat = jnp.where(strict, kk * b_c * decay, 0.0)
            chains.append(dict(ci=ci, d=d, q=q, k=k, v=v, qk=qk, g_c=g_c, b_c=b_c, g_last=g_last, mask=mask,
                               decay=decay, n_mat=n_mat, x=eye_f - jnp.where(same_blk[1], n_mat, 0.0)))
    for lv in range(1, n_levels):
        for ch in chains:
            ch["xb"] = ch["x"].astype(BF16)
            ch["a"] = jnp.dot(ch["xb"], jnp.where(off_blk[lv], ch["n_mat"], 0.0).astype(BF16),
                              preferred_element_type=F32)
        for ch in chains:
            ch["x"] = ch["x"] - jnp.dot(ch["a"].astype(BF16), ch["xb"], preferred_element_type=F32)
    for ch in chains:
        kf = ch["k"].astype(F32)
        ch["e_g"] = jnp.exp(ch["g_c"])
        kb = kf * ch["b_c"]
        rhs = jnp.concatenate([(kb * ch["e_g"]).astype(BF16),
                               (ch["v"].astype(F32) * ch["b_c"]).astype(BF16)], axis=1)
        ch["wu"] = jnp.dot(ch["x"].astype(BF16), rhs, preferred_element_type=F32).astype(BF16)
        ch["attn"] = jnp.where(ch["mask"], ch["qk"] * ch["decay"], 0.0).astype(BF16)
        ch["ke_t"] = (kf * jnp.exp(ch["g_last"] - ch["g_c"])).T.astype(BF16)
    for ch in chains:
        ch["m1"] = jnp.dot(ch["ke_t"], ch["wu"], preferred_element_type=F32)
        ch["m2"] = jnp.dot(ch["attn"], ch["wu"], preferred_element_type=F32)
    for ch in chains:
        ci, d, m1, m2 = ch["ci"], ch["d"], ch["m1"], ch["m2"]
        aq_ref[d, 0, ci, 0:c, :] = (-m1[:, 0:B_DV]).astype(BF16)
        aq_ref[d, 0, ci, c:2 * c, :] = (ch["q"].astype(F32) * ch["e_g"] - m2[:, 0:B_DV]).astype(BF16)
        bo_ref[d, 0, ci, 0:c, :] = m1[:, B_DV:].astype(BF16)
        bo_ref[d, 0, ci, c:2 * c, :] = m2[:, B_DV:].astype(BF16)
    for ci in range(n_chunks):
        e_f, e_b = (jnp.broadcast_to(jnp.exp(chains[2 * ci + d]["g_last"]), (1, c)) for d in range(2))
        e_ref[0, ci] = jnp.concatenate([e_f, e_b, jnp.zeros((6, c), F32)], axis=0)


def _delta_prep(qkv, g4):
    t = qkv.shape[0]
    c = DELTA_CHUNK
    nch = t // c
    g = _tile(nch, 11, 6)
    rows = g * c
    return pl.pallas_call(
        functools.partial(_delta_prep_kernel, n_chunks=g),
        grid=(B_HEADS, nch // g),
        in_specs=[pl.BlockSpec((rows, LANES), lambda h, i: (i, h)),
                  pl.BlockSpec((rows, LANES), lambda h, i: (i, B_HEADS + h)),
                  pl.BlockSpec((rows, LANES), lambda h, i: (i, 2 * B_HEADS + h)),
                  pl.BlockSpec((1, g, 8, c), lambda h, i: (h, i, 0, 0))],
        out_specs=[pl.BlockSpec((2, 1, g, 2 * c, B_DV), lambda h, i: (0, h, i, 0, 0)),
                   pl.BlockSpec((2, 1, g, 2 * c, B_DV), lambda h, i: (0, h, i, 0, 0)),
                   pl.BlockSpec((1, g, 8, c), lambda h, i: (h, i, 0, 0))],
        out_shape=[jax.ShapeDtypeStruct((2, B_HEADS, nch, 2 * c, B_DV), BF16),
                   jax.ShapeDtypeStruct((2, B_HEADS, nch, 2 * c, B_DV), BF16),
                   jax.ShapeDtypeStruct((B_HEADS, nch, 8, c), F32)],
        compiler_params=_params("parallel", "parallel"),
    )(qkv, qkv, qkv, g4)


def _delta_scan_kernel(aqf_ref, bof_ref, ef_ref, aqb_ref, bob_ref, eb_ref, of_ref, ob_ref, st_ref):
    c = DELTA_CHUNK

    @pl.when(pl.program_id(0) == 0)
    def _():
        st_ref[...] = jnp.zeros_like(st_ref)

    for d, (aq_ref, bo_ref, e_ref, o_ref) in enumerate(((aqf_ref, bof_ref, ef_ref, of_ref),
                                                        (aqb_ref, bob_ref, eb_ref, ob_ref))):
        for h in range(B_HEADS):
            state = st_ref[d, h]
            res = jnp.dot(aq_ref[0, h, 0], state.astype(BF16), preferred_element_type=F32)
            bo = bo_ref[0, h, 0].astype(F32)
            o_ref[:, h * B_DV:(h + 1) * B_DV] = res[c:] + bo[c:]
            st_ref[d, h] = state * e_ref[h, 0, d:d + 1, 0:1] + res[0:c] + bo[0:c]


def _delta_scan(aq, bo, ee, n_lat, n_ctx):
    c = DELTA_CHUNK
    nch = aq.shape[2]
    n_lat_c = n_lat // c
    n_ctx_c = n_ctx // c
    t = nch * c

    def fwd(s):
        return jnp.where(s < n_ctx_c, s + n_lat_c, s - n_ctx_c)

    def bwd(s):
        return nch - 1 - s

    def big(direction, order):
        return pl.BlockSpec((1, B_HEADS, 1, 2 * c, B_DV), lambda s: (direction, 0, order(s), 0, 0))

    def small(order):
        return pl.BlockSpec((B_HEADS, 1, 8, c), lambda s: (0, order(s), 0, 0))

    return pl.pallas_call(
        _delta_scan_kernel,
        grid=(nch,),
        in_specs=[big(0, fwd), big(0, fwd), small(fwd), big(1, bwd), big(1, bwd), small(bwd)],
        out_specs=[pl.BlockSpec((c, V_B), lambda s: (fwd(s), 0)),
                   pl.BlockSpec((c, V_B), lambda s: (bwd(s), 0))],
        out_shape=[jax.ShapeDtypeStruct((t, V_B), F32), jax.ShapeDtypeStruct((t, V_B), F32)],
        scratch_shapes=[pltpu.VMEM((2, B_HEADS, B_DK, B_DV), F32)],
        compiler_params=_params("arbitrary"),
    )(aq, bo, ee, aq, bo, ee)


def _delta_gated(of_ref, ob_ref, z_ref, nw_ref):
    nw = nw_ref[...]
    heads = []
    for h in range(B_HEADS):
        sl = slice(h * B_DV, (h + 1) * B_DV)
        o = of_ref[:, sl] + ob_ref[:, sl]
        ms = jnp.mean(o * o, axis=-1, keepdims=True)
        y = (o * lax.rsqrt(ms + NORM_EPS) * nw) * _silu(z_ref[:, sl].astype(F32))
        heads.append(y.astype(BF16))
    return jnp.concatenate(heads, axis=1)


def _out_kernel(ya_ref, yac_ref, of_ref, ob_ref, z_ref, dnw_ref, wa_ref, wb_ref, x_ref, mod_ref, nw_ref, modn_ref,
                o_ref, xn_ref, *, n_lat_tiles):
    is_ctx = pl.program_id(0) >= n_lat_tiles
    d = x_ref.shape[1]
    gate = jnp.where(is_ctx, mod_ref[1:2, :], mod_ref[0:1, :])
    ya = jnp.where(is_ctx, yac_ref[...], ya_ref[...])
    acc = jnp.dot(ya, wa_ref[0], preferred_element_type=F32)
    acc = acc + jnp.dot(_delta_gated(of_ref, ob_ref, z_ref, dnw_ref), wb_ref[0], preferred_element_type=F32)
    x = x_ref[...] + gate * acc
    o_ref[...] = x
    shift = jnp.where(is_ctx, modn_ref[1:2, 0:d], modn_ref[0:1, 0:d])
    scale = jnp.where(is_ctx, modn_ref[1:2, d:2 * d], modn_ref[0:1, d:2 * d])
    xn_ref[...] = _modulated_norm(x, nw_ref[...], shift, scale).astype(xn_ref.dtype)


def _out_final_kernel(ya_ref, of_ref, ob_ref, z_ref, dnw_ref, wa_ref, wb_ref, x_ref, mod_ref, fw_ref, o_ref):
    acc = jnp.dot(ya_ref[...], wa_ref[0], preferred_element_type=F32)
    acc = acc + jnp.dot(_delta_gated(of_ref, ob_ref, z_ref, dnw_ref), wb_ref[0], preferred_element_type=F32)
    x = x_ref[...] + mod_ref[0:1, :] * acc
    ms = jnp.mean(x * x, axis=-1, keepdims=True)
    o_ref[...] = x * lax.rsqrt(ms + NORM_EPS) * fw_ref[...]


def _delta_out_specs(tm):
    return [pl.BlockSpec((tm, V_B), lambda i: (i, 0)),
            pl.BlockSpec((tm, V_B), lambda i: (i, 0)),
            pl.BlockSpec((tm, V_B), lambda i: (i, V_A // V_B)),
            pl.BlockSpec((1, B_DV), lambda i: (0, 0))]


def _out_proj(ya, ya_ctx, out_f, out_b, zz, dn_norm_w, w_out, layer, xcat, mod, nw_next, mod_next, n_lat):
    t, d = xcat.shape
    tm = 256
    n_lat_tiles = n_lat // tm
    return pl.pallas_call(
        functools.partial(_out_kernel, n_lat_tiles=n_lat_tiles),
        grid=(t // tm,),
        in_specs=[pl.BlockSpec((tm, V_A), lambda i: (jnp.minimum(i, n_lat_tiles - 1), 0)),
                  pl.BlockSpec((tm, V_A), lambda i: (jnp.maximum(i - n_lat_tiles, 0), 0)),
                  *_delta_out_specs(tm),
                  pl.BlockSpec((1, V_A, d), lambda i: (layer, 0, 0)),
                  pl.BlockSpec((1, V_B, d), lambda i: (layer, V_A // V_B, 0)),
                  pl.BlockSpec((tm, d), lambda i: (i, 0)),
                  pl.BlockSpec((8, d), lambda i: (0, 2)),
                  pl.BlockSpec((1, d), lambda i: (0, 0)),
                  pl.BlockSpec((8, 2 * d), lambda i: (0, 0))],
        out_specs=[pl.BlockSpec((tm, d), lambda i: (i, 0)),
                   pl.BlockSpec((tm, d), lambda i: (i, 0))],
        out_shape=[jax.ShapeDtypeStruct((t, d), F32), jax.ShapeDtypeStruct((t, d), BF16)],
        compiler_params=_params("parallel"),
    )(ya, ya_ctx, out_f, out_b, zz, dn_norm_w, w_out, w_out, xcat, mod, nw_next, mod_next)


def _out_proj_final(ya, out_f, out_b, zz, dn_norm_w, w_out, layer, xcat, mod, final_w, n_lat):
    d = xcat.shape[1]
    tm = 256
    return pl.pallas_call(
        _out_final_kernel,
        grid=(n_lat // tm,),
        in_specs=[pl.BlockSpec((tm, V_A), lambda i: (i, 0)),
                  *_delta_out_specs(tm),
                  pl.BlockSpec((1, V_A, d), lambda i: (layer, 0, 0)),
                  pl.BlockSpec((1, V_B, d), lambda i: (layer, V_A // V_B, 0)),
                  pl.BlockSpec((tm, d), lambda i: (i, 0)),
                  pl.BlockSpec((8, d), lambda i: (0, 2)),
                  pl.BlockSpec((1, d), lambda i: (0, 0))],
        out_specs=pl.BlockSpec((tm, d), lambda i: (i, 0)),
        out_shape=jax.ShapeDtypeStruct((n_lat, d), F32),
        compiler_params=_params("parallel"),
    )(ya, out_f, out_b, zz, dn_norm_w, w_out, w_out, xcat, mod, final_w)


def _rope_tables(n_lat, n_ctx):
    n_rows = n_lat // GRID_W
    rows = jnp.repeat(jnp.arange(n_rows, dtype=F32), GRID_W)
    cols = jnp.tile(jnp.arange(GRID_W, dtype=F32), n_rows)
    n_freq = A_HEAD_DIM // 4
    inv = ROPE_THETA ** (-jnp.arange(n_freq, dtype=F32) / n_freq)
    ang = jnp.concatenate([rows[:, None] * inv, cols[:, None] * inv], axis=-1)
    cos = jnp.cos(ang)
    sin = jnp.sin(ang)
    cos_t = jnp.tile(cos, (1, LANES // (A_HEAD_DIM // 2)))
    sin_t = jnp.tile(jnp.concatenate([-sin, sin], axis=-1), (1, LANES // A_HEAD_DIM))
    cos_t = jnp.concatenate([cos_t, jnp.ones((n_ctx, LANES), F32)], axis=0)
    sin_t = jnp.concatenate([sin_t, jnp.zeros((n_ctx, LANES), F32)], axis=0)
    return cos_t, sin_t


def kernel(x, c, ctx, c_ctx, w_ada, b_ada, norm_w, w_in, conv_w, dn_a_log, dn_dt_bias, dn_norm_w,
           diff_lambda, subln_w, w_out, final_norm_w):
    depth = w_ada.shape[0]
    n_lat = x.shape[1]
    n_ctx = ctx.shape[1]
    t = n_lat + n_ctx
    d = x.shape[2]
    assert x.shape[0] == 1 and d == D_MODEL
    assert n_lat % 512 == 0 and n_ctx == 256 and t % 768 == 0

    xcat = jnp.concatenate([x[0], ctx[0]], axis=0)
    cc = jnp.zeros((8, d), F32).at[0].set(c[0]).at[1].set(c_ctx)
    mods = _ada_all(cc, w_ada, b_ada)
    cos_t, sin_t = _rope_tables(n_lat, n_ctx)

    o_q, o_v, o_z, o_b, o_zb, o_ab = 0, 2 * QK_A, 2 * QK_A + V_A, 2 * QK_A + 2 * V_A, \
        2 * QK_A + 2 * V_A + CONV_CH, 2 * QK_A + 2 * V_A + CONV_CH + V_B
    w_in_bf = w_in[:, :, :o_ab].astype(BF16)
    w_out_bf = w_out.astype(BF16)
    w_ab = jnp.pad(w_in[:, :, o_ab:], ((0, 0), (0, 0), (0, LANES - 4 * B_HEADS))).astype(BF16)

    nch = t // DELTA_CHUNK
    xn = _norm_mod(xcat, norm_w[0][None, :], mods[0], n_lat)
    for i in range(depth):
        last = i == depth - 1
        lam_init = 0.8 - 0.6 * math.exp(-0.3 * i)
        mod = mods[i]
        qk = _proj(xn, w_in_bf, i, [(o_q, 2 * QK_A)], BF16, 256, rope_q_cols=QK_A, rope=(cos_t, sin_t))
        vt = _proj(xn, w_in_bf, i, [(o_v, V_A)], BF16, 512, transposed=True)
        zz = _proj(xn, w_in_bf, i, [(o_z, V_A), (o_zb, V_B), (o_b, CONV_CH)], BF16, 512)
        ab = _proj(xn, w_ab, i, [(0, LANES)], F32, LANES)

        lam_p = diff_lambda[i]
        sw = subln_w[i][None, :]
        ya = _attention(qk, zz, vt, lam_p, sw, lam_init, q_row0=0, n_q=n_lat, k_row0=0, n_keys=t,
                        tq=_tile(n_lat, 1024, 512), tk=768)
        if not last:
            ya_ctx = _attention(qk, zz, vt, lam_p, sw, lam_init, q_row0=n_lat, n_q=n_ctx, k_row0=n_lat,
                                n_keys=n_ctx, tq=256, tk=256)

        qkv = _conv_prep(zz, V_A + V_B, conv_w[i], n_lat, n_ctx)
        ab_t = ab[:, 0:4 * B_HEADS].T
        alog_b = jnp.broadcast_to(dn_a_log[i].reshape(2 * B_HEADS, 1), (2 * B_HEADS, DELTA_CHUNK))
        dtb_b = jnp.broadcast_to(dn_dt_bias[i].reshape(2 * B_HEADS, 1), (2 * B_HEADS, DELTA_CHUNK))
        gates = _gates(ab_t, alog_b, dtb_b)
        g4 = gates.reshape(4, B_HEADS, nch, DELTA_CHUNK).transpose(1, 2, 0, 3)
        g4 = jnp.concatenate([g4, jnp.zeros_like(g4)], axis=2)
        aq, bo, ee = _delta_prep(qkv, g4)
        out_f, out_b = _delta_scan(aq, bo, ee, n_lat, n_ctx)
        dnw = dn_norm_w[i][None, :]

        if not last:
            xcat, xn = _out_proj(ya, ya_ctx, out_f, out_b, zz, dnw, w_out_bf, i, xcat, mod,
                                 norm_w[i + 1][None, :], mods[i + 1], n_lat)
        else:
            out = _out_proj_final(ya, out_f, out_b, zz, dnw, w_out_bf, i, xcat, mod, final_norm_w[None, :], n_lat)
    return out[None]
```

```python
import functools
import math

import jax
import jax.numpy as jnp
from jax import lax
from jax.experimental import pallas as pl
from jax.experimental.pallas import tpu as pltpu

F32 = jnp.float32
BF16 = jnp.bfloat16

LANES = 128
VMEM_LIMIT_BYTES = 56 * 1024 * 1024

D_MODEL = 2048
GRID_W = 64
A_HEADS = 8
A_HEAD_DIM = 64
A_VAL_DIM = 2 * A_HEAD_DIM
QK_A = A_HEADS * 2 * A_HEAD_DIM
V_A = A_HEADS * A_VAL_DIM
B_HEADS = 8
B_DK = 128
B_DV = 128
QK_B = B_HEADS * B_DK
V_B = B_HEADS * B_DV
CONV_K = 5
CONV_CH = 2 * QK_B + V_B
ROPE_THETA = 10000.0
NORM_EPS = 1e-6
SUBLN_EPS = 1e-5
L2_EPS = 1e-6

DELTA_CHUNK = 128
V_ROWS = A_VAL_DIM + 16
NEG_BIG = -1e30
SCAN_STEPS = 2
SCORE_LOOKAHEAD = 2
Q_SCALE = A_HEAD_DIM ** -0.5 * math.log2(math.e)


def _params(*sem, flags=None):
    return pltpu.CompilerParams(dimension_semantics=sem, vmem_limit_bytes=VMEM_LIMIT_BYTES, flags=flags)


def _silu(x):
    return x * jax.nn.sigmoid(x)


def _tile(n, preferred, fallback):
    return preferred if n % preferred == 0 else fallback


def _ada_kernel(c_ref, w_ref, b_ref, o_ref):
    s = _silu(c_ref[...])
    o_ref[0] = jnp.dot(s.astype(BF16), w_ref[0].astype(BF16), preferred_element_type=F32) + b_ref[0]


def _ada_all(cc, w_ada, b_ada):
    depth, d, n = w_ada.shape
    tn = 512
    return pl.pallas_call(
        _ada_kernel,
        grid=(depth, n // tn),
        in_specs=[pl.BlockSpec((8, d), lambda l, j: (0, 0)),
                  pl.BlockSpec((1, d, tn), lambda l, j: (l, 0, j)),
                  pl.BlockSpec((1, 1, tn), lambda l, j: (l, 0, j))],
        out_specs=pl.BlockSpec((1, 8, tn), lambda l, j: (l, 0, j)),
        out_shape=jax.ShapeDtypeStruct((depth, 8, n), F32),
        compiler_params=_params("parallel", "parallel"),
    )(cc, w_ada, b_ada.reshape(depth, 1, n))


def _modulated_norm(x, nw, shift, scale):
    ms = jnp.mean(x * x, axis=-1, keepdims=True)
    return (x * lax.rsqrt(ms + NORM_EPS) * nw) * (1.0 + scale) + shift


def _norm_kernel(x_ref, nw_ref, mod_ref, o_ref, *, n_lat_tiles):
    is_ctx = pl.program_id(0) >= n_lat_tiles
    d = x_ref.shape[1]
    shift = jnp.where(is_ctx, mod_ref[1:2, 0:d], mod_ref[0:1, 0:d])
    scale = jnp.where(is_ctx, mod_ref[1:2, d:2 * d], mod_ref[0:1, d:2 * d])
    o_ref[...] = _modulated_norm(x_ref[...], nw_ref[...], shift, scale).astype(o_ref.dtype)


def _norm_mod(xcat, nw, mod, n_lat):
    t, d = xcat.shape
    tm = 256
    return pl.pallas_call(
        functools.partial(_norm_kernel, n_lat_tiles=n_lat // tm),
        grid=(t // tm,),
        in_specs=[pl.BlockSpec((tm, d), lambda i: (i, 0)),
                  pl.BlockSpec((1, d), lambda i: (0, 0)),
                  pl.BlockSpec((8, 2 * d), lambda i: (0, 0))],
        out_specs=pl.BlockSpec((tm, d), lambda i: (i, 0)),
        out_shape=jax.ShapeDtypeStruct((t, d), BF16),
        compiler_params=_params("parallel"),
    )(xcat, nw, mod)


def _mm_kernel(x_ref, w_ref, o_ref):
    o_ref[...] = jnp.dot(x_ref[...], w_ref[0], preferred_element_type=F32).astype(o_ref.dtype)


def _mm_t_kernel(x_ref, w_ref, o_ref, acc_ref):
    acc_ref[...] = jnp.dot(x_ref[...], w_ref[0], preferred_element_type=F32)
    o_ref[...] = acc_ref[...].T.astype(o_ref.dtype)


def _mm_rope_kernel(x_ref, w_ref, cos_ref, sin_ref, o_ref, *, q_tiles):
    scale = jnp.where(pl.program_id(1) < q_tiles, Q_SCALE, 1.0)
    t = jnp.dot(x_ref[...], w_ref[0], preferred_element_type=F32) * scale
    c = cos_ref[...]
    s = sin_ref[...]
    lane = lax.broadcasted_iota(jnp.int32, c.shape, 1)
    first = (lane % A_HEAD_DIM) < (A_HEAD_DIM // 2)
    for a in range(0, t.shape[1], LANES):
        th = t[:, a:a + LANES]
        sw = jnp.where(first, pltpu.roll(th, LANES - A_HEAD_DIM // 2, 1), pltpu.roll(th, A_HEAD_DIM // 2, 1))
        o_ref[:, a:a + LANES] = (th * c + sw * s).astype(o_ref.dtype)


def _proj(xn, w, layer, col_segments, out_dtype, tn, rope_q_cols=None, rope=None, transposed=False):
    t, d = xn.shape
    tm = _tile(t, 1408, 256)
    assert all(c0 % tn == 0 and width % tn == 0 for c0, width in col_segments)
    n_tiles = sum(width // tn for _, width in col_segments)
    scratch = []
    bounds, offs, acc = [], [], 0
    for c0, width in col_segments:
        offs.append(c0 // tn - acc)
        acc += width // tn
        bounds.append(acc)

    def wcol(j):
        col = j + offs[-1]
        for bnd, off in zip(reversed(bounds[:-1]), reversed(offs[:-1])):
            col = jnp.where(j < bnd, j + off, col)
        return col

    in_specs = [pl.BlockSpec((tm, d), lambda i, j: (i, 0)),
                pl.BlockSpec((1, d, tn), lambda i, j: (layer, 0, wcol(j)))]
    args = [xn, w]
    kern = _mm_kernel
    out_spec = pl.BlockSpec((tm, tn), lambda i, j: (i, j))
    out_shape = (t, n_tiles * tn)
    if rope is not None:
        in_specs += [pl.BlockSpec((tm, LANES), lambda i, j: (i, 0))] * 2
        args += list(rope)
        kern = functools.partial(_mm_rope_kernel, q_tiles=rope_q_cols // tn)
    if transposed:
        kern = _mm_t_kernel
        out_spec = pl.BlockSpec((tn, tm), lambda i, j: (j, i))
        out_shape = (n_tiles * tn, t)
        scratch = [pltpu.VMEM((tm, tn), F32)]
    return pl.pallas_call(
        kern,
        grid=(t // tm, n_tiles),
        in_specs=in_specs,
        out_specs=out_spec,
        out_shape=jax.ShapeDtypeStruct(out_shape, out_dtype),
        scratch_shapes=scratch,
        compiler_params=_params("parallel", "parallel"),
    )(*args)


def _attn_kernel(q_ref, k_ref, vt_ref, z_ref, lam_ref, sw_ref, o_ref, *, tk, lam_init):
    tq = q_ref.shape[0]
    nkeys = k_ref.shape[0]
    q = q_ref[...]
    lane = lax.broadcasted_iota(jnp.int32, q.shape, 1)
    zero = jnp.zeros_like(q)
    qz = jnp.concatenate([jnp.where(lane < A_HEAD_DIM, q, zero),
                          jnp.where(lane >= A_HEAD_DIM, q, zero)], axis=0)
    m = jnp.full((1, 2 * tq), NEG_BIG, F32)
    acc = jnp.zeros((V_ROWS, 2 * tq), F32)
    tail_row = lax.broadcasted_iota(jnp.int32, (V_ROWS - A_VAL_DIM, tk), 0)
    tail = jnp.where(tail_row == 0, 1.0, 0.0).astype(BF16)

    def scores(c0):
        return lax.dot_general(k_ref[c0:c0 + tk, :], qz, (((1,), (1,)), ((), ())),
                               preferred_element_type=F32)

    starts = list(range(0, nkeys, tk))
    queue = [scores(c0) for c0 in starts[:SCORE_LOOKAHEAD]]
    for idx, c0 in enumerate(starts):
        st = queue.pop(0)
        if idx + SCORE_LOOKAHEAD < len(starts):
            queue.append(scores(starts[idx + SCORE_LOOKAHEAD]))
        m_new = jnp.maximum(m, jnp.max(st, axis=0, keepdims=True))
        alpha = jnp.exp2(m - m_new)
        p = jnp.exp2(st - m_new).astype(BF16)
        v_ext = jnp.concatenate([vt_ref[:, c0:c0 + tk], tail], axis=0)
        acc = alpha * acc + jnp.dot(v_ext, p, preferred_element_type=F32)
        m = m_new
    o1 = acc[0:A_VAL_DIM, 0:tq] / acc[A_VAL_DIM:A_VAL_DIM + 1, 0:tq]
    o2 = acc[0:A_VAL_DIM, tq:] / acc[A_VAL_DIM:A_VAL_DIM + 1, tq:]
    lp = lam_ref[...]
    lam = (jnp.exp(jnp.sum(lp[0:1] * lp[1:2], axis=1, keepdims=True))
           - jnp.exp(jnp.sum(lp[2:3] * lp[3:4], axis=1, keepdims=True)) + lam_init)
    o = (o1 - lam * o2).T
    ms = jnp.mean(o * o, axis=-1, keepdims=True)
    y = (o * lax.rsqrt(ms + SUBLN_EPS) * sw_ref[...]) * (1.0 - lam_init)
    o_ref[...] = (y * _silu(z_ref[...].astype(F32))).astype(o_ref.dtype)


def _attention(qk, zz, vt, lam_p, subln_w, lam_init, *, q_row0, n_q, k_row0, n_keys, tq, tk):
    qb0 = q_row0 // tq
    kb0 = k_row0 // n_keys
    return pl.pallas_call(
        functools.partial(_attn_kernel, tk=tk, lam_init=lam_init),
        grid=(A_HEADS, n_q // tq),
        in_specs=[pl.BlockSpec((tq, LANES), lambda h, i: (qb0 + i, h)),
                  pl.BlockSpec((n_keys, LANES), lambda h, i: (kb0, A_HEADS + h)),
                  pl.BlockSpec((A_VAL_DIM, n_keys), lambda h, i: (h, kb0)),
                  pl.BlockSpec((tq, LANES), lambda h, i: (qb0 + i, h)),
                  pl.BlockSpec(lam_p.shape, lambda h, i: (0, 0)),
                  pl.BlockSpec((1, A_VAL_DIM), lambda h, i: (0, 0))],
        out_specs=pl.BlockSpec((tq, LANES), lambda h, i: (i, h)),
        out_shape=jax.ShapeDtypeStruct((n_q, V_A), BF16),
        compiler_params=_params("parallel", "parallel"),
    )(qk, qk, vt, zz, lam_p, subln_w)


def _conv_kernel(x_ref, w_ref, o_ref, pad_ref, *, n_lat, n_ctx, rows):
    j = pl.program_id(0)
    half = (CONV_K - 1) // 2
    lat0 = 8
    ctx0 = lat0 + n_lat + 16
    zeros8 = jnp.zeros((8, LANES), F32)
    pad_ref[0:8, :] = zeros8
    pad_ref[lat0 + n_lat:lat0 + n_lat + 8, :] = zeros8
    pad_ref[lat0 + n_lat + 8:ctx0, :] = zeros8
    pad_ref[ctx0 + n_ctx:ctx0 + n_ctx + 8, :] = zeros8
    pad_ref[lat0:lat0 + n_lat, :] = x_ref[0:n_lat, :].astype(F32)
    pad_ref[ctx0:ctx0 + n_ctx, :] = x_ref[n_lat:n_lat + n_ctx, :].astype(F32)
    w = w_ref[...]
    is_qk = j < 2 * B_HEADS
    post = jnp.where(j < B_HEADS, B_DK ** -0.5, 1.0)
    for src0, dst0, n in ((lat0, 0, n_lat), (ctx0, n_lat, n_ctx)):
        for r in range(0, n, rows):
            acc = jnp.zeros((rows, LANES), F32)
            for tap in range(CONV_K):
                a = src0 + r + tap - half
                acc = acc + pad_ref[a:a + rows, :] * w[tap:tap + 1, :]
            y = _silu(acc)
            nrm = lax.rsqrt(jnp.sum(y * y, axis=-1, keepdims=True) + L2_EPS) * post
            y = y * jnp.where(is_qk, nrm, 1.0)
            o_ref[dst0 + r:dst0 + r + rows, :] = y.astype(o_ref.dtype)


def _conv_prep(pb, col0, conv_w, n_lat, n_ctx):
    t = pb.shape[0]
    nblk = CONV_CH // LANES
    blk0 = col0 // LANES
    return pl.pallas_call(
        functools.partial(_conv_kernel, n_lat=n_lat, n_ctx=n_ctx, rows=256),
        grid=(nblk,),
        in_specs=[pl.BlockSpec((t, LANES), lambda j: (0, blk0 + j)),
                  pl.BlockSpec((CONV_K, LANES), lambda j: (0, j))],
        out_specs=pl.BlockSpec((t, LANES), lambda j: (0, j)),
        out_shape=jax.ShapeDtypeStruct((t, CONV_CH), BF16),
        scratch_shapes=[pltpu.VMEM((t + 40, LANES), F32)],
        compiler_params=_params("parallel"),
    )(pb, conv_w)


def _gates_kernel(ab_ref, alog_ref, dtb_ref, o_ref):
    c = DELTA_CHUNK
    nh = B_HEADS
    r = lax.broadcasted_iota(jnp.int32, (c, c), 0)
    l = lax.broadcasted_iota(jnp.int32, (c, c), 1)
    upper = (r <= l).astype(F32)
    lower = (r >= l).astype(F32)
    neg_a = -jnp.exp(alog_ref[...])
    dtb = dtb_ref[...]
    for t0 in range(0, ab_ref.shape[1], c):
        x = ab_ref[0:2 * nh, t0:t0 + c] + dtb
        sp = jnp.maximum(x, 0.0) + jnp.log1p(jnp.exp(-jnp.abs(x)))
        g = neg_a * sp
        o_ref[0:nh, t0:t0 + c] = jnp.dot(g[0:nh], upper, preferred_element_type=F32,
                                          precision=lax.Precision.HIGHEST)
        o_ref[nh:2 * nh, t0:t0 + c] = jnp.dot(g[nh:2 * nh], lower, preferred_element_type=F32,
                                               precision=lax.Precision.HIGHEST)
        o_ref[2 * nh:4 * nh, t0:t0 + c] = jax.nn.sigmoid(ab_ref[2 * nh:4 * nh, t0:t0 + c])


def _gates(ab_t, alog_b, dtb_b):
    rows, t = ab_t.shape
    tl = _tile(t, 1408, 256)
    return pl.pallas_call(
        _gates_kernel,
        grid=(t // tl,),
        in_specs=[pl.BlockSpec((rows, tl), lambda i: (0, i)),
                  pl.BlockSpec(alog_b.shape, lambda i: (0, 0)),
                  pl.BlockSpec(dtb_b.shape, lambda i: (0, 0))],
        out_specs=pl.BlockSpec((rows, tl), lambda i: (0, i)),
        out_shape=jax.ShapeDtypeStruct((rows, t), F32),
        compiler_params=_params("parallel"),
    )(ab_t, alog_b, dtb_b)


def _delta_prep_kernel(q_ref, k_ref, v_ref, g_ref, aq_ref, bo_ref, e_ref, *, n_chunks):
    c = DELTA_CHUNK
    row = lax.broadcasted_iota(jnp.int32, (c, c), 0)
    col = lax.broadcasted_iota(jnp.int32, (c, c), 1)
    eye = row == col
    eye_f = eye.astype(F32)
    n_levels = int(math.log2(c))
    same_blk = [(row >> s) == (col >> s) for s in range(n_levels + 1)]
    off_blk = [None] + [jnp.logical_and(same_blk[lv + 1], jnp.logical_not(same_blk[lv]))
                        for lv in range(1, n_levels)]
    nt = (((1,), (1,)), ((), ()))

    def to_col(r):
        return jnp.sum(jnp.where(eye, jnp.broadcast_to(r, (c, c)), 0.0), axis=1, keepdims=True)

    chains = []
    for ci in range(n_chunks):
        q = q_ref[ci * c:(ci + 1) * c, :]
        k = k_ref[ci * c:(ci + 1) * c, :]
        v = v_ref[ci * c:(ci + 1) * c, :]
        rows = g_ref[0, ci]
        kk = lax.dot_general(k, k, nt, preferred_element_type=F32)
        qk = lax.dot_general(q, k, nt, preferred_element_type=F32)
        for d in range(2):
            if d == 0:
                g_r, b_r = rows[0:1], rows[2:3]
                g_last = g_r[:, c - 1:c]
                mask, strict = row >= col, row > col
            else:
                g_r, b_r = rows[1:2], rows[3:4]
                g_last = g_r[:, 0:1]
                mask, strict = row <= col, row < col
            g_c = to_col(g_r)
            b_c = to_col(b_r)
            decay = jnp.where(mask, jnp.exp(jnp.where(mask, g_c - g_r, 0.0)), 0.0)
            n_mat = jnp.where(strict, kk * b_c * decay, 0.0)
            chains.append(dict(ci=ci, d=d, q=q, k=k, v=v, qk=qk, g_c=g_c, b_c=b_c, g_last=g_last, mask=mask,
                               decay=decay, n_mat=n_mat, x=eye_f - jnp.where(same_blk[1], n_mat, 0.0)))
    for lv in range(1, n_levels):
        for ch in chains:
            ch["xb"] = ch["x"].astype(BF16)
            ch["a"] = jnp.dot(ch["xb"], jnp.where(off_blk[lv], ch["n_mat"], 0.0).astype(BF16),
                              preferred_element_type=F32)
        for ch in chains:
            ch["x"] = ch["x"] - jnp.dot(ch["a"].astype(BF16), ch["xb"], preferred_element_type=F32)
    for ch in chains:
        kf = ch["k"].astype(F32)
        ch["e_g"] = jnp.exp(ch["g_c"])
        kb = kf * ch["b_c"]
        rhs = jnp.concatenate([(kb * ch["e_g"]).astype(BF16),
                               (ch["v"].astype(F32) * ch["b_c"]).astype(BF16)], axis=1)
        ch["wu"] = jnp.dot(ch["x"].astype(BF16), rhs, preferred_element_type=F32).astype(BF16)
        ch["attn"] = jnp.where(ch["mask"], ch["qk"] * ch["decay"], 0.0).astype(BF16)
        ch["ke_t"] = (kf * jnp.exp(ch["g_last"] - ch["g_c"])).T.astype(BF16)
    for ch in chains:
        ch["m1"] = jnp.dot(ch["ke_t"], ch["wu"], preferred_element_type=F32)
        ch["m2"] = jnp.dot(ch["attn"], ch["wu"], preferred_element_type=F32)
    for ch in chains:
        ci, d, m1, m2 = ch["ci"], ch["d"], ch["m1"], ch["m2"]
        aq_ref[d, 0, ci, 0:c, :] = (-m1[:, 0:B_DV]).astype(BF16)
        aq_ref[d, 0, ci, c:2 * c, :] = (ch["q"].astype(F32) * ch["e_g"] - m2[:, 0:B_DV]).astype(BF16)
        bo_ref[d, 0, ci, 0:c, :] = m1[:, B_DV:].astype(BF16)
        bo_ref[d, 0, ci, c:2 * c, :] = m2[:, B_DV:].astype(BF16)
    for ci in range(n_chunks):
        e_f, e_b = (jnp.broadcast_to(jnp.exp(chains[2 * ci + d]["g_last"]), (1, c)) for d in range(2))
        e_ref[0, ci] = jnp.concatenate([e_f, e_b, jnp.zeros((6, c), F32)], axis=0)


def _delta_prep(qkv, g4):
    t = qkv.shape[0]
    c = DELTA_CHUNK
    nch = t // c
    g = _tile(nch, 11, 6)
    rows = g * c
    return pl.pallas_call(
        functools.partial(_delta_prep_kernel, n_chunks=g),
        grid=(B_HEADS, nch // g),
        in_specs=[pl.BlockSpec((rows, LANES), lambda h, i: (i, h)),
                  pl.BlockSpec((rows, LANES), lambda h, i: (i, B_HEADS + h)),
                  pl.BlockSpec((rows, LANES), lambda h, i: (i, 2 * B_HEADS + h)),
                  pl.BlockSpec((1, g, 8, c), lambda h, i: (h, i, 0, 0))],
        out_specs=[pl.BlockSpec((2, 1, g, 2 * c, B_DV), lambda h, i: (0, h, i, 0, 0)),
                   pl.BlockSpec((2, 1, g, 2 * c, B_DV), lambda h, i: (0, h, i, 0, 0)),
                   pl.BlockSpec((1, g, 8, c), lambda h, i: (h, i, 0, 0))],
        out_shape=[jax.ShapeDtypeStruct((2, B_HEADS, nch, 2 * c, B_DV), BF16),
                   jax.ShapeDtypeStruct((2, B_HEADS, nch, 2 * c, B_DV), BF16),
                   jax.ShapeDtypeStruct((B_HEADS, nch, 8, c), F32)],
        compiler_params=_params("parallel", "parallel"),
    )(qkv, qkv, qkv, g4)


def _delta_scan_kernel(aqf_ref, bof_ref, ef_ref, aqb_ref, bob_ref, eb_ref, of_ref, ob_ref, st_ref):
    c = DELTA_CHUNK

    @pl.when(pl.program_id(0) == 0)
    def _():
        st_ref[...] = jnp.zeros_like(st_ref)

    for sub in range(SCAN_STEPS):
        for d, (aq_ref, bo_ref, e_ref, o_ref) in enumerate(((aqf_ref, bof_ref, ef_ref, of_ref),
                                                            (aqb_ref, bob_ref, eb_ref, ob_ref))):
            ci = sub if d == 0 else SCAN_STEPS - 1 - sub
            for h in range(B_HEADS):
                state = st_ref[d, h]
                res = jnp.dot(aq_ref[0, h, ci], state.astype(BF16), preferred_element_type=F32)
                bo = bo_ref[0, h, ci].astype(F32)
                o_ref[ci * c:(ci + 1) * c, h * B_DV:(h + 1) * B_DV] = res[c:] + bo[c:]
                st_ref[d, h] = state * e_ref[h, ci, d:d + 1, 0:1] + res[0:c] + bo[0:c]


def _delta_scan(aq, bo, ee, n_lat, n_ctx):
    c = DELTA_CHUNK
    nch = aq.shape[2]
    k = SCAN_STEPS
    assert n_lat % (k * c) == 0 and n_ctx % (k * c) == 0
    n_lat_b = n_lat // (k * c)
    n_ctx_b = n_ctx // (k * c)
    nblk = nch // k
    t = nch * c

    def fwd(s):
        return jnp.where(s < n_ctx_b, s + n_lat_b, s - n_ctx_b)

    def bwd(s):
        return nblk - 1 - s

    def big(direction, order):
        return pl.BlockSpec((1, B_HEADS, k, 2 * c, B_DV), lambda s: (direction, 0, order(s), 0, 0))

    def small(order):
        return pl.BlockSpec((B_HEADS, k, 8, c), lambda s: (0, order(s), 0, 0))

    return pl.pallas_call(
        _delta_scan_kernel,
        grid=(nblk,),
        in_specs=[big(0, fwd), big(0, fwd), small(fwd), big(1, bwd), big(1, bwd), small(bwd)],
        out_specs=[pl.BlockSpec((k * c, V_B), lambda s: (fwd(s), 0)),
                   pl.BlockSpec((k * c, V_B), lambda s: (bwd(s), 0))],
        out_shape=[jax.ShapeDtypeStruct((t, V_B), F32), jax.ShapeDtypeStruct((t, V_B), F32)],
        scratch_shapes=[pltpu.VMEM((2, B_HEADS, B_DK, B_DV), F32)],
        compiler_params=_params("arbitrary"),
    )(aq, bo, ee, aq, bo, ee)


def _delta_gated(of_ref, ob_ref, z_ref, nw_ref):
    nw = nw_ref[...]
    heads = []
    for h in range(B_HEADS):
        sl = slice(h * B_DV, (h + 1) * B_DV)
        o = of_ref[:, sl] + ob_ref[:, sl]
        ms = jnp.mean(o * o, axis=-1, keepdims=True)
        y = (o * lax.rsqrt(ms + NORM_EPS) * nw) * _silu(z_ref[:, sl].astype(F32))
        heads.append(y.astype(BF16))
    return jnp.concatenate(heads, axis=1)


def _out_kernel(ya_ref, yac_ref, of_ref, ob_ref, z_ref, dnw_ref, wa_ref, wb_ref, x_ref, mod_ref, nw_ref, modn_ref,
                o_ref, xn_ref, *, n_lat_tiles):
    is_ctx = pl.program_id(0) >= n_lat_tiles
    d = x_ref.shape[1]
    gate = jnp.where(is_ctx, mod_ref[1:2, :], mod_ref[0:1, :])
    ya = jnp.where(is_ctx, yac_ref[...], ya_ref[...])
    acc = jnp.dot(ya, wa_ref[0], preferred_element_type=F32)
    acc = acc + jnp.dot(_delta_gated(of_ref, ob_ref, z_ref, dnw_ref), wb_ref[0], preferred_element_type=F32)
    x = x_ref[...] + gate * acc
    o_ref[...] = x
    shift = jnp.where(is_ctx, modn_ref[1:2, 0:d], modn_ref[0:1, 0:d])
    scale = jnp.where(is_ctx, modn_ref[1:2, d:2 * d], modn_ref[0:1, d:2 * d])
    xn_ref[...] = _modulated_norm(x, nw_ref[...], shift, scale).astype(xn_ref.dtype)


def _out_final_kernel(ya_ref, of_ref, ob_ref, z_ref, dnw_ref, wa_ref, wb_ref, x_ref, mod_ref, fw_ref, o_ref):
    acc = jnp.dot(ya_ref[...], wa_ref[0], preferred_element_type=F32)
    acc = acc + jnp.dot(_delta_gated(of_ref, ob_ref, z_ref, dnw_ref), wb_ref[0], preferred_element_type=F32)
    x = x_ref[...] + mod_ref[0:1, :] * acc
    ms = jnp.mean(x * x, axis=-1, keepdims=True)
    o_ref[...] = x * lax.rsqrt(ms + NORM_EPS) * fw_ref[...]


def _delta_out_specs(tm):
    return [pl.BlockSpec((tm, V_B), lambda i: (i, 0)),
            pl.BlockSpec((tm, V_B), lambda i: (i, 0)),
            pl.BlockSpec((tm, V_B), lambda i: (i, V_A // V_B)),
            pl.BlockSpec((1, B_DV), lambda i: (0, 0))]


def _out_proj(ya, ya_ctx, out_f, out_b, zz, dn_norm_w, w_out, layer, xcat, mod, nw_next, mod_next, n_lat):
    t, d = xcat.shape
    tm = 256
    n_lat_tiles = n_lat // tm
    return pl.pallas_call(
        functools.partial(_out_kernel, n_lat_tiles=n_lat_tiles),
        grid=(t // tm,),
        in_specs=[pl.BlockSpec((tm, V_A), lambda i: (jnp.minimum(i, n_lat_tiles - 1), 0)),
                  pl.BlockSpec((tm, V_A), lambda i: (jnp.maximum(i - n_lat_tiles, 0), 0)),
                  *_delta_out_specs(tm),
                  pl.BlockSpec((1, V_A, d), lambda i: (layer, 0, 0)),
                  pl.BlockSpec((1, V_B, d), lambda i: (layer, V_A // V_B, 0)),
                  pl.BlockSpec((tm, d), lambda i: (i, 0)),
                  pl.BlockSpec((8, d), lambda i: (0, 2)),
                  pl.BlockSpec((1, d), lambda i: (0, 0)),
                  pl.BlockSpec((8, 2 * d), lambda i: (0, 0))],
        out_specs=[pl.BlockSpec((tm, d), lambda i: (i, 0)),
                   pl.BlockSpec((tm, d), lambda i: (i, 0))],
        out_shape=[jax.ShapeDtypeStruct((t, d), F32), jax.ShapeDtypeStruct((t, d), BF16)],
        compiler_params=_params("parallel"),
    )(ya, ya_ctx, out_f, out_b, zz, dn_norm_w, w_out, w_out, xcat, mod, nw_next, mod_next)


def _out_proj_final(ya, out_f, out_b, zz, dn_norm_w, w_out, layer, xcat, mod, final_w, n_lat):
    d = xcat.shape[1]
    tm = 256
    return pl.pallas_call(
        _out_final_kernel,
        grid=(n_lat // tm,),
        in_specs=[pl.BlockSpec((tm, V_A), lambda i: (i, 0)),
                  *_delta_out_specs(tm),
                  pl.BlockSpec((1, V_A, d), lambda i: (layer, 0, 0)),
                  pl.BlockSpec((1, V_B, d), lambda i: (layer, V_A // V_B, 0)),
                  pl.BlockSpec((tm, d), lambda i: (i, 0)),
                  pl.BlockSpec((8, d), lambda i: (0, 2)),
                  pl.BlockSpec((1, d), lambda i: (0, 0))],
        out_specs=pl.BlockSpec((tm, d), lambda i: (i, 0)),
        out_shape=jax.ShapeDtypeStruct((n_lat, d), F32),
        compiler_params=_params("parallel"),
    )(ya, out_f, out_b, zz, dn_norm_w, w_out, w_out, xcat, mod, final_w)


def _rope_tables(n_lat, n_ctx):
    n_rows = n_lat // GRID_W
    rows = jnp.repeat(jnp.arange(n_rows, dtype=F32), GRID_W)
    cols = jnp.tile(jnp.arange(GRID_W, dtype=F32), n_rows)
    n_freq = A_HEAD_DIM // 4
    inv = ROPE_THETA ** (-jnp.arange(n_freq, dtype=F32) / n_freq)
    ang = jnp.concatenate([rows[:, None] * inv, cols[:, None] * inv], axis=-1)
    cos = jnp.cos(ang)
    sin = jnp.sin(ang)
    cos_t = jnp.tile(cos, (1, LANES // (A_HEAD_DIM // 2)))
    sin_t = jnp.tile(jnp.concatenate([-sin, sin], axis=-1), (1, LANES // A_HEAD_DIM))
    cos_t = jnp.concatenate([cos_t, jnp.ones((n_ctx, LANES), F32)], axis=0)
    sin_t = jnp.concatenate([sin_t, jnp.zeros((n_ctx, LANES), F32)], axis=0)
    return cos_t, sin_t


def kernel(x, c, ctx, c_ctx, w_ada, b_ada, norm_w, w_in, conv_w, dn_a_log, dn_dt_bias, dn_norm_w,
           diff_lambda, subln_w, w_out, final_norm_w):
    depth = w_ada.shape[0]
    n_lat = x.shape[1]
    n_ctx = ctx.shape[1]
    t = n_lat + n_ctx
    d = x.shape[2]
    assert x.shape[0] == 1 and d == D_MODEL
    assert n_lat % 512 == 0 and n_ctx == 256 and t % 768 == 0

    xcat = jnp.concatenate([x[0], ctx[0]], axis=0)
    cc = jnp.zeros((8, d), F32).at[0].set(c[0]).at[1].set(c_ctx)
    mods = _ada_all(cc, w_ada, b_ada)
    cos_t, sin_t = _rope_tables(n_lat, n_ctx)

    o_q, o_v, o_z, o_b, o_zb, o_ab = 0, 2 * QK_A, 2 * QK_A + V_A, 2 * QK_A + 2 * V_A, \
        2 * QK_A + 2 * V_A + CONV_CH, 2 * QK_A + 2 * V_A + CONV_CH + V_B
    w_in_bf = w_in[:, :, :o_ab].astype(BF16)
    w_out_bf = w_out.astype(BF16)
    w_ab = jnp.pad(w_in[:, :, o_ab:], ((0, 0), (0, 0), (0, LANES - 4 * B_HEADS))).astype(BF16)

    nch = t // DELTA_CHUNK
    xn = _norm_mod(xcat, norm_w[0][None, :], mods[0], n_lat)
    for i in range(depth):
        last = i == depth - 1
        lam_init = 0.8 - 0.6 * math.exp(-0.3 * i)
        mod = mods[i]
        qk = _proj(xn, w_in_bf, i, [(o_q, 2 * QK_A)], BF16, 256, rope_q_cols=QK_A, rope=(cos_t, sin_t))
        vt = _proj(xn, w_in_bf, i, [(o_v, V_A)], BF16, 512, transposed=True)
        zz = _proj(xn, w_in_bf, i, [(o_z, V_A), (o_zb, V_B), (o_b, CONV_CH)], BF16, 512)
        ab = _proj(xn, w_ab, i, [(0, LANES)], F32, LANES)

        lam_p = diff_lambda[i]
        sw = subln_w[i][None, :]
        ya = _attention(qk, zz, vt, lam_p, sw, lam_init, q_row0=0, n_q=n_lat, k_row0=0, n_keys=t,
                        tq=_tile(n_lat, 1024, 512), tk=768)
        if not last:
            ya_ctx = _attention(qk, zz, vt, lam_p, sw, lam_init, q_row0=n_lat, n_q=n_ctx, k_row0=n_lat,
                                n_keys=n_ctx, tq=256, tk=256)

        qkv = _conv_prep(zz, V_A + V_B, conv_w[i], n_lat, n_ctx)
        ab_t = ab[:, 0:4 * B_HEADS].T
        alog_b = jnp.broadcast_to(dn_a_log[i].reshape(2 * B_HEADS, 1), (2 * B_HEADS, DELTA_CHUNK))
        dtb_b = jnp.broadcast_to(dn_dt_bias[i].reshape(2 * B_HEADS, 1), (2 * B_HEADS, DELTA_CHUNK))
        gates = _gates(ab_t, alog_b, dtb_b)
        g4 = gates.reshape(4, B_HEADS, nch, DELTA_CHUNK).transpose(1, 2, 0, 3)
        g4 = jnp.concatenate([g4, jnp.zeros_like(g4)], axis=2)
        aq, bo, ee = _delta_prep(qkv, g4)
        out_f, out_b = _delta_scan(aq, bo, ee, n_lat, n_ctx)
        dnw = dn_norm_w[i][None, :]

        if not last:
            xcat, xn = _out_proj(ya, ya_ctx, out_f, out_b, zz, dnw, w_out_bf, i, xcat, mod,
                                 norm_w[i + 1][None, :], mods[i + 1], n_lat)
        else:
            out = _out_proj_final(ya, out_f, out_b, zz, dnw, w_out_bf, i, xcat, mod, final_norm_w[None, :], n_lat)
    return out[None]
```
